```python
import math
import jax, jax.numpy as jnp
from jax import lax
import numpy as np

D_MODEL = 1024
BATCH = 8
SEQ = 4096
DEPTH = 1

GRID_W = 64
N_MEM = 256
D_HYENA = 512
HYENA_GROUPS = 8
HYENA_ORDER = 2
SHORT_CONV = 3
FILTER_BANDS = 16
FILTER_EMB = 1 + 2 * FILTER_BANDS
FILTER_FF = 64
DECAY_TARGET = 1e-2
FAST_DECAY_PCT = 0.3
SLOW_DECAY_PCT = 1.5
N_Q_HEADS = 8
N_KV_HEADS = 2
HEAD_DIM = 64
D_ATTN = N_Q_HEADS * HEAD_DIM
D_KV = N_KV_HEADS * HEAD_DIM
Q_BLOCK = 128
ROPE_THETA = 10000.0
D_MIX = D_HYENA + D_ATTN
D_IN = (HYENA_ORDER + 1) * D_HYENA + D_ATTN + 2 * D_KV
MEM_HEADS = 4
MEM_HEAD_DIM = 128
D_MEM_ATTN = MEM_HEADS * MEM_HEAD_DIM
N_GROUPS = 4
EXPERTS_PER_GROUP = 4
N_EXPERTS = N_GROUPS * EXPERTS_PER_GROUP
TOP_K = 2
D_EXPERT = 512
EPS = 1e-6

kernel_name = 'hybrid_hyena_gqa_hmoe_encoder'


def rms_norm(x, g):
    xf = x.astype(jnp.float32)
    y = xf * lax.rsqrt(jnp.mean(xf * xf, axis=-1, keepdims=True) + EPS)
    return (y * g.astype(jnp.float32)).astype(x.dtype)


def group_rms_norm(x, g, n_groups):
    shp = x.shape
    xg = x.reshape(shp[:-1] + (n_groups, shp[-1] // n_groups))
    return rms_norm(xg, g.reshape(n_groups, -1)).reshape(shp)


def rope_section(x, pos):
    m = x.shape[-1] // 2
    inv = ROPE_THETA ** (-jnp.arange(m, dtype=jnp.float32) / m)
    ang = pos.astype(jnp.float32)[:, None] * inv[None, :]
    cos = jnp.cos(ang)[None, :, None, :]
    sin = jnp.sin(ang)[None, :, None, :]
    xf = x.astype(jnp.float32)
    x1, x2 = xf[..., :m], xf[..., m:]
    return jnp.concatenate([x1 * cos - x2 * sin, x2 * cos + x1 * sin], axis=-1).astype(x.dtype)


def rope_2d(x, row, col):
    half = x.shape[-1] // 2
    return jnp.concatenate([rope_section(x[..., :half], row), rope_section(x[..., half:], col)], axis=-1)


def hyena_filter_spectra(L, w1, b1, w2, b2, w3, sin_freq):
    f32 = jnp.float32
    t = jnp.linspace(0.0, 1.0, L, dtype=f32)[:, None]
    bands = jnp.linspace(1e-4, FILTER_BANDS - 1, FILTER_BANDS, dtype=f32)[None, :]
    w = (2.0 * math.pi / L) * jnp.arange(L, dtype=f32)[:, None]
    z = jnp.concatenate([t, jnp.cos(bands * w), jnp.sin(bands * w)], axis=-1)
    a = jnp.sin(sin_freq[0].astype(f32) * (z @ w1.astype(f32) + b1.astype(f32)))
    a = jnp.sin(sin_freq[1].astype(f32) * (a @ w2.astype(f32) + b2.astype(f32)))
    filt = (a @ w3.astype(f32)).reshape(L, HYENA_ORDER, 2, D_HYENA)
    deltas = jnp.abs(jnp.linspace(math.log(DECAY_TARGET) / SLOW_DECAY_PCT,
                                  math.log(DECAY_TARGET) / FAST_DECAY_PCT, D_HYENA, dtype=f32))
    filt = filt * jnp.exp(-t * deltas)[:, None, None, :]
    fwd, bwd = filt[:, :, 0], filt[:, :, 1]
    k2 = jnp.concatenate([fwd, jnp.zeros_like(fwd[:1]), bwd[:0:-1]], axis=0)
    k2 = k2 / (jnp.sum(jnp.abs(k2), axis=0, keepdims=True) + EPS)
    return jnp.fft.rfft(k2, axis=0)


def fft_long_conv(z, k_f):
    L = z.shape[1]
    zf = jnp.fft.rfft(z.astype(jnp.float32), n=2 * L, axis=1)
    return jnp.fft.irfft(zf * k_f[None], n=2 * L, axis=1)[:, :L]


def hyena_mixer(u, short_w, short_b, skip, k_f):
    C = u.shape[-1]
    u = lax.conv_general_dilated(u, short_w[:, None, :].astype(u.dtype), window_strides=(1,),
                                 padding=((SHORT_CONV // 2, SHORT_CONV // 2),),
                                 dimension_numbers=('NWC', 'WIO', 'NWC'),
                                 feature_group_count=C) + short_b.astype(u.dtype)
    v, g1, g2 = jnp.split(u, 3, axis=-1)
    z = v
    for o, gate in enumerate((g1, g2)):
        zc = fft_long_conv(z, k_f[:, o]) + skip[o].astype(jnp.float32) * z.astype(jnp.float32)
        z = gate * zc.astype(gate.dtype)
    return z


def gqa_mixer(q, k, v, q_norm, k_norm, row, col):
    B, L, _ = q.shape
    G = N_Q_HEADS // N_KV_HEADS
    q = rope_2d(rms_norm(q.reshape(B, L, N_Q_HEADS, HEAD_DIM), q_norm), row, col)
    k = rope_2d(rms_norm(k.reshape(B, L, N_KV_HEADS, HEAD_DIM), k_norm), row, col)
    v = v.reshape(B, L, N_KV_HEADS, HEAD_DIM)
    nb = L // Q_BLOCK
    qb = q.reshape(B, nb, Q_BLOCK, N_KV_HEADS, G, HEAD_DIM).transpose(1, 0, 3, 4, 2, 5)
    kt = k.transpose(0, 2, 1, 3)
    vt = v.transpose(0, 2, 1, 3)
    scale = HEAD_DIM ** -0.5

    def attend_block(qblk):
        s = jnp.einsum('bkgqd,bksd->bkgqs', qblk, kt).astype(jnp.float32) * scale
        p = jax.nn.softmax(s, axis=-1).astype(vt.dtype)
        return jnp.einsum('bkgqs,bksd->bkgqd', p, vt)

    o = lax.map(attend_block, qb)
    return o.transpose(1, 0, 4, 2, 3, 5).reshape(B, L, D_ATTN)


def memory_cross_attn(h, m, w_q, w_k, w_v, q_norm, k_norm, w_o):
    B, L, _ = h.shape
    M = m.shape[1]
    q = rms_norm((h @ w_q).reshape(B, L, MEM_HEADS, MEM_HEAD_DIM), q_norm)
    k = rms_norm((m @ w_k).reshape(B, M, MEM_HEADS, MEM_HEAD_DIM), k_norm)
    v = (m @ w_v).reshape(B, M, MEM_HEADS, MEM_HEAD_DIM)
    s = jnp.einsum('blhd,bmhd->bhlm', q, k).astype(jnp.float32) * (MEM_HEAD_DIM ** -0.5)
    p = jax.nn.softmax(s, axis=-1).astype(v.dtype)
    o = jnp.einsum('bhlm,bmhd->blhd', p, v).reshape(B, L, D_MEM_ATTN)
    return o @ w_o


def hier_moe(h, w_rg, b_rg, w_re, b_re, w_gate, w_up, w_down):
    B, L, D = h.shape
    hf = h.reshape(B * L, D)
    grp_prob = jax.nn.softmax((hf @ w_rg).astype(jnp.float32) + b_rg.astype(jnp.float32), axis=-1)
    p_grp, g_idx = lax.top_k(grp_prob, 1)
    exp_logits = ((hf @ w_re).astype(jnp.float32) + b_re.astype(jnp.float32)).reshape(
        -1, N_GROUPS, EXPERTS_PER_GROUP)
    sel_logits = jnp.take_along_axis(exp_logits, g_idx[:, :, None], axis=1)[:, 0]
    w_top, e_loc = lax.top_k(jax.nn.softmax(sel_logits, axis=-1), TOP_K)
    w_top = p_grp * w_top / jnp.sum(w_top, axis=-1, keepdims=True)
    e_glob = g_idx * EXPERTS_PER_GROUP + e_loc
    gates = jnp.einsum('tk,tke->te', w_top,
                       jax.nn.one_hot(e_glob, N_EXPERTS, dtype=jnp.float32)).astype(h.dtype)
    y = jnp.zeros_like(hf)
    for e in range(N_EXPERTS):
        a = jax.nn.silu(hf @ w_gate[e]) * (hf @ w_up[e])
        y = y + gates[:, e:e + 1] * (a @ w_down[e])
    return y.reshape(B, L, D)


def setup_inputs(seed: int = 0) -> dict:
    key = jax.random.key(seed)
    ks = iter(jax.random.split(key, 40))
    f32 = jnp.float32

    def nrm(shape, scale):
        return jax.random.normal(next(ks), shape, f32) * scale

    def gain(shape):
        return 1.0 + 0.05 * jax.random.normal(next(ks), shape, f32)

    return {
        'x': nrm((BATCH, SEQ, D_MODEL), 1.0),
        'mem': nrm((BATCH, N_MEM, D_MODEL), 1.0),
        'norm_mix': gain((DEPTH, D_MODEL)),
        'w_in': nrm((DEPTH, D_MODEL, D_IN), D_MODEL ** -0.5),
        'hyena_short_w': nrm((DEPTH, SHORT_CONV, (HYENA_ORDER + 1) * D_HYENA), SHORT_CONV ** -0.5),
        'hyena_short_b': nrm((DEPTH, (HYENA_ORDER + 1) * D_HYENA), 0.02),
        'filt_w1': nrm((DEPTH, FILTER_EMB, FILTER_FF), FILTER_EMB ** -0.5),
        'filt_b1': nrm((DEPTH, FILTER_FF), 0.02),
        'filt_w2': nrm((DEPTH, FILTER_FF, FILTER_FF), FILTER_FF ** -0.5),
        'filt_b2': nrm((DEPTH, FILTER_FF), 0.02),
        'filt_w3': nrm((DEPTH, FILTER_FF, HYENA_ORDER * 2 * D_HYENA), FILTER_FF ** -0.5),
        'filt_sin_freq': gain((DEPTH, 2, FILTER_FF)),
        'hyena_skip': nrm((DEPTH, HYENA_ORDER, D_HYENA), 0.1),
        'attn_q_norm': gain((DEPTH, HEAD_DIM)),
        'attn_k_norm': gain((DEPTH, HEAD_DIM)),
        'out_norm_hyena': gain((DEPTH, D_HYENA)),
        'out_norm_attn': gain((DEPTH, D_ATTN)),
        'w_out': nrm((DEPTH, D_MIX, D_MODEL), D_MIX ** -0.5),
        'norm_xattn': gain((DEPTH, D_MODEL)),
        'norm_mem': gain((DEPTH, D_MODEL)),
        'w_q_mem': nrm((DEPTH, D_MODEL, D_MEM_ATTN), D_MODEL ** -0.5),
        'w_k_mem': nrm((DEPTH, D_MODEL, D_MEM_ATTN), D_MODEL ** -0.5),
        'w_v_mem': nrm((DEPTH, D_MODEL, D_MEM_ATTN), D_MODEL ** -0.5),
        'mem_q_norm': gain((DEPTH, MEM_HEAD_DIM)),
        'mem_k_norm': gain((DEPTH, MEM_HEAD_DIM)),
        'w_o_mem': nrm((DEPTH, D_MEM_ATTN, D_MODEL), D_MEM_ATTN ** -0.5),
        'norm_moe': gain((DEPTH, D_MODEL)),
        'w_router_grp': nrm((DEPTH, D_MODEL, N_GROUPS), D_MODEL ** -0.5),
        'b_router_grp': nrm((DEPTH, N_GROUPS), 0.01),
        'w_router_exp': nrm((DEPTH, D_MODEL, N_EXPERTS), D_MODEL ** -0.5),
        'b_router_exp': nrm((DEPTH, N_EXPERTS), 0.01),
        'w_gate': nrm((DEPTH, N_EXPERTS, D_MODEL, D_EXPERT), D_MODEL ** -0.5),
        'w_up': nrm((DEPTH, N_EXPERTS, D_MODEL, D_EXPERT), D_MODEL ** -0.5),
        'w_down': nrm((DEPTH, N_EXPERTS, D_EXPERT, D_MODEL), D_EXPERT ** -0.5),
    }


def reference(x, mem, norm_mix, w_in, hyena_short_w, hyena_short_b, filt_w1, filt_b1, filt_w2,
              filt_b2, filt_w3, filt_sin_freq, hyena_skip, attn_q_norm, attn_k_norm,
              out_norm_hyena, out_norm_attn, w_out, norm_xattn, norm_mem, w_q_mem, w_k_mem,
              w_v_mem, mem_q_norm, mem_k_norm, w_o_mem, norm_moe, w_router_grp, b_router_grp,
              w_router_exp, b_router_exp, w_gate, w_up, w_down):
    L = x.shape[1]
    rows = L // GRID_W
    row = jnp.broadcast_to(jnp.arange(rows, dtype=jnp.int32)[:, None], (rows, GRID_W)).reshape(-1)
    col = jnp.broadcast_to(jnp.arange(GRID_W, dtype=jnp.int32)[None, :], (rows, GRID_W)).reshape(-1)
    split_at = [(HYENA_ORDER + 1) * D_HYENA,
                (HYENA_ORDER + 1) * D_HYENA + D_ATTN,
                (HYENA_ORDER + 1) * D_HYENA + D_ATTN + D_KV]
    for i in range(DEPTH):
        h = rms_norm(x, norm_mix[i])
        u_hy, q, k, v = jnp.split(h @ w_in[i], split_at, axis=-1)
        k_f = hyena_filter_spectra(L, filt_w1[i], filt_b1[i], filt_w2[i], filt_b2[i], filt_w3[i],
                                   filt_sin_freq[i])
        y_hy = hyena_mixer(u_hy, hyena_short_w[i], hyena_short_b[i], hyena_skip[i], k_f)
        y_at = gqa_mixer(q, k, v, attn_q_norm[i], attn_k_norm[i], row, col)
        y = jnp.concatenate([group_rms_norm(y_hy, out_norm_hyena[i], HYENA_GROUPS),
                             group_rms_norm(y_at, out_norm_attn[i], N_Q_HEADS)], axis=-1)
        x = x + y @ w_out[i]
        x = x + memory_cross_attn(rms_norm(x, norm_xattn[i]), rms_norm(mem, norm_mem[i]),
                                  w_q_mem[i], w_k_mem[i], w_v_mem[i], mem_q_norm[i], mem_k_norm[i],
                                  w_o_mem[i])
        x = x + hier_moe(rms_norm(x, norm_moe[i]), w_router_grp[i], b_router_grp[i], w_router_exp[i],
                         b_router_exp[i], w_gate[i], w_up[i], w_down[i])
    return x
```

```python
import functools
import math

import jax
import jax.numpy as jnp
import numpy as np
from jax import lax
from jax.experimental import pallas as pl
from jax.experimental.pallas import tpu as pltpu

F32 = jnp.float32
BF16 = jnp.bfloat16

D_MODEL = 1024
GRID_W = 64
D_HYENA = 512
HYENA_GROUPS = 8
HYENA_ORDER = 2
SHORT_CONV = 3
FILTER_BANDS = 16
DECAY_TARGET = 1e-2
FAST_DECAY_PCT = 0.3
SLOW_DECAY_PCT = 1.5
N_Q_HEADS = 8
N_KV_HEADS = 2
HEAD_DIM = 64
D_ATTN = N_Q_HEADS * HEAD_DIM
D_KV = N_KV_HEADS * HEAD_DIM
ROPE_THETA = 10000.0
D_U = (HYENA_ORDER + 1) * D_HYENA
D_IN = D_U + D_ATTN + 2 * D_KV
D_QK = D_ATTN + D_KV
MEM_HEADS = 4
MEM_HEAD_DIM = 128
D_MEM_ATTN = MEM_HEADS * MEM_HEAD_DIM
N_GROUPS = 4
EXPERTS_PER_GROUP = 4
N_EXPERTS = N_GROUPS * EXPERTS_PER_GROUP
D_EXPERT = 512
EPS = 1e-6

PAIRS = ((0, 1), (0, 2), (0, 3), (1, 2), (1, 3), (2, 3))
N_CLASSES = N_GROUPS * len(PAIRS)
ROUTE_ROWS = 32
LANE = 128
ROW_W = D_MODEL + LANE
VMEM_LIMIT = 48 * 1024 * 1024


def _rms(x):
    return x * lax.rsqrt(jnp.mean(x * x, axis=-1, keepdims=True) + EPS)


def _group_sumsq_matrix(width, group):
    idx = np.arange(width) // group
    return jnp.asarray((idx[:, None] == idx[None, :]).astype(np.float32) / group, dtype=BF16)


def _inproj_kernel(x_ref, g_ref, w_ref, cos_ref, sin_ref, qkg_ref, gmat_ref,
                   u_ref, q_ref, k_ref, v_ref):
    tm = x_ref.shape[0]
    h = _rms(x_ref[...]) * g_ref[...]
    p = jnp.dot(h.astype(BF16), w_ref[...], preferred_element_type=F32)
    u_ref[...] = p[:, :D_U]
    qk = p[:, D_U:D_U + D_QK]
    ms = jnp.dot((qk * qk).astype(BF16), gmat_ref[...], preferred_element_type=F32)
    qk = qk * lax.rsqrt(ms + EPS) * qkg_ref[...]
    reps = D_QK // LANE
    cos = jnp.concatenate([cos_ref[...]] * reps, axis=1)
    sin = jnp.concatenate([sin_ref[...]] * reps, axis=1)
    lane = lax.broadcasted_iota(jnp.int32, (tm, D_QK), 1)
    partner = jnp.where((lane % 32) < 16, pltpu.roll(qk, D_QK - 16, 1), pltpu.roll(qk, 16, 1))
    qk = qk * cos + partner * sin
    for hd in range(N_Q_HEADS):
        q_ref[0, hd] = (qk[:, hd * HEAD_DIM:(hd + 1) * HEAD_DIM] * (HEAD_DIM ** -0.5)).astype(BF16)
    for hd in range(N_KV_HEADS):
        k_ref[0, hd] = qk[:, D_ATTN + hd * HEAD_DIM:D_ATTN + (hd + 1) * HEAD_DIM].astype(BF16)
    vv = p[:, D_U + D_QK:]
    lane_v = lax.broadcasted_iota(jnp.int32, (tm, LANE), 1)
    ones_col = jnp.where(lane_v == HEAD_DIM, 1.0, 0.0)
    v_ref[0, 0] = jnp.where(lane_v < HEAD_DIM, vv, ones_col).astype(BF16)
    v_ref[0, 1] = jnp.where(lane_v < HEAD_DIM, pltpu.roll(vv, HEAD_DIM, 1), ones_col).astype(BF16)


def _inproj(x, g, w_bf, cos_t, sin_t, qkg, tm):
    B, L, D = x.shape
    nt = L // tm
    gmat = _group_sumsq_matrix(D_QK, HEAD_DIM)
    xf = x.reshape(B * L, D)
    return pl.pallas_call(
        _inproj_kernel,
        grid=(B * nt,),
        in_specs=[
            pl.BlockSpec((tm, D), lambda i: (i, 0)),
            pl.BlockSpec((1, D), lambda i: (0, 0)),
            pl.BlockSpec((D, D_IN), lambda i: (0, 0)),
            pl.BlockSpec((tm, LANE), lambda i: (i % nt, 0)),
            pl.BlockSpec((tm, LANE), lambda i: (i % nt, 0)),
            pl.BlockSpec((1, D_QK), lambda i: (0, 0)),
            pl.BlockSpec((D_QK, D_QK), lambda i: (0, 0)),
        ],
        out_specs=[
            pl.BlockSpec((tm, D_U), lambda i: (i, 0)),
            pl.BlockSpec((1, N_Q_HEADS, tm, HEAD_DIM), lambda i: (i // nt, 0, i % nt, 0)),
            pl.BlockSpec((1, N_KV_HEADS, tm, HEAD_DIM), lambda i: (i // nt, 0, i % nt, 0)),
            pl.BlockSpec((1, N_KV_HEADS, tm, LANE), lambda i: (i // nt, 0, i % nt, 0)),
        ],
        out_shape=[
            jax.ShapeDtypeStruct((B * L, D_U), F32),
            jax.ShapeDtypeStruct((B, N_Q_HEADS, L, HEAD_DIM), BF16),
            jax.ShapeDtypeStruct((B, N_KV_HEADS, L, HEAD_DIM), BF16),
            jax.ShapeDtypeStruct((B, N_KV_HEADS, L, LANE), BF16),
        ],
        compiler_params=pltpu.CompilerParams(dimension_semantics=("parallel",),
                                             vmem_limit_bytes=VMEM_LIMIT),
        name="inproj",
    )(xf, g, w_bf, cos_t, sin_t, qkg, gmat)


def _rope_tables(L):
    pos = np.arange(L)
    row, col = pos // GRID_W, pos % GRID_W
    d = np.arange(LANE) % HEAD_DIM
    sec, r = d // 32, d % 32
    inv = ROPE_THETA ** (-(r % 16).astype(np.float64) / 16.0)
    p = np.where(sec[None, :] == 0, row[:, None], col[:, None]).astype(np.float64)
    ang = p.astype(np.float32) * inv.astype(np.float32)[None, :]
    sign = np.where(r < 16, -1.0, 1.0)[None, :]
    return jnp.asarray(np.cos(ang), F32), jnp.asarray(np.sin(ang) * sign, F32)


def _attn_kernel(q_ref, k_ref, v_ref, g_ref, o_ref, *, tk):
    G, tq = q_ref.shape[1], q_ref.shape[2]
    L = k_ref.shape[2]
    R = G * tq
    q = q_ref[0].reshape(R, HEAD_DIM)
    m = jnp.full((R, 1), -jnp.inf, F32)
    acc = jnp.zeros((R, LANE), F32)
    for c in range(L // tk):
        kc = k_ref[0, 0, c * tk:(c + 1) * tk, :]
        s = lax.dot_general(q, kc, (((1,), (1,)), ((), ())), preferred_element_type=F32)
        m_new = jnp.maximum(m, jnp.max(s, axis=-1, keepdims=True))
        p = jnp.exp(s - m_new)
        acc = jnp.exp(m - m_new) * acc + jnp.dot(p.astype(BF16), v_ref[0, 0, c * tk:(c + 1) * tk, :],
                                                  preferred_element_type=F32)
        m = m_new
    o = acc[:, :HEAD_DIM] / acc[:, HEAD_DIM:HEAD_DIM + 1]
    o = _rms(o)
    o = jnp.concatenate([o[h * tq:(h + 1) * tq] for h in range(G)], axis=1)
    o_ref[0] = (o * g_ref[...]).astype(o_ref.dtype)


def _attention(q, k, v, g_out, tq, tk):
    B, H, L, _ = q.shape
    G = H // N_KV_HEADS
    return pl.pallas_call(
        functools.partial(_attn_kernel, tk=tk),
        grid=(B, N_KV_HEADS, L // tq),
        in_specs=[
            pl.BlockSpec((1, G, tq, HEAD_DIM), lambda b, kv, i: (b, kv, i, 0)),
            pl.BlockSpec((1, 1, L, HEAD_DIM), lambda b, kv, i: (b, kv, 0, 0)),
            pl.BlockSpec((1, 1, L, LANE), lambda b, kv, i: (b, kv, 0, 0)),
            pl.BlockSpec((1, G * HEAD_DIM), lambda b, kv, i: (0, kv)),
        ],
        out_specs=pl.BlockSpec((1, tq, G * HEAD_DIM), lambda b, kv, i: (b, i, kv)),
        out_shape=jax.ShapeDtypeStruct((B, L, D_ATTN), BF16),
        compiler_params=pltpu.CompilerParams(dimension_semantics=("parallel", "parallel", "parallel"),
                                             vmem_limit_bytes=VMEM_LIMIT),
        name="attention",
    )(q, k, v, g_out)


def _memkv_kernel(m_ref, g_ref, wk_ref, wv_ref, kg_ref, k_ref, v_ref):
    h = (_rms(m_ref[0]) * g_ref[...]).astype(BF16)
    k = jnp.dot(h, wk_ref[...], preferred_element_type=F32)
    v = jnp.dot(h, wv_ref[...], preferred_element_type=F32)
    kn = [_rms(k[:, i * MEM_HEAD_DIM:(i + 1) * MEM_HEAD_DIM]) * kg_ref[...] for i in range(MEM_HEADS)]
    k_ref[0] = jnp.concatenate(kn, axis=1).astype(BF16)
    v_ref[0] = v.astype(BF16)


def _memkv(mem, g, wk_bf, wv_bf, kg):
    B, M, D = mem.shape
    return pl.pallas_call(
        _memkv_kernel,
        grid=(B,),
        in_specs=[
            pl.BlockSpec((1, M, D), lambda b: (b, 0, 0)),
            pl.BlockSpec((1, D), lambda b: (0, 0)),
            pl.BlockSpec((D, D_MEM_ATTN), lambda b: (0, 0)),
            pl.BlockSpec((D, D_MEM_ATTN), lambda b: (0, 0)),
            pl.BlockSpec((1, MEM_HEAD_DIM), lambda b: (0, 0)),
        ],
        out_specs=[pl.BlockSpec((1, M, D_MEM_ATTN), lambda b: (b, 0, 0))] * 2,
        out_shape=[jax.ShapeDtypeStruct((B, M, D_MEM_ATTN), BF16)] * 2,
        compiler_params=pltpu.CompilerParams(dimension_semantics=("parallel",),
                                             vmem_limit_bytes=VMEM_LIMIT),
        name="memkv",
    )(mem, g, wk_bf, wv_bf, kg)


def _split_bf16(a):
    hi = a.astype(BF16)
    return hi, (a - hi.astype(F32)).astype(BF16)


def _mix_kernel(x_ref, yh_ref, ya_ref, gh_ref, gmat_ref, wo_ref, gx_ref, wq_ref, qg_ref,
                km_ref, vm_ref, wom_ref, gmoe_ref, wrh_ref, wrl_ref, br_ref, tri_ref,
                xw_ref, route_ref, cnt_ref):
    tm = x_ref.shape[0]
    yh = yh_ref[...]
    ms = jnp.dot((yh * yh).astype(BF16), gmat_ref[...], preferred_element_type=F32)
    yh = yh * lax.rsqrt(ms + EPS) * gh_ref[...]
    x1 = (x_ref[...]
          + jnp.dot(yh.astype(BF16), wo_ref[:D_HYENA, :], preferred_element_type=F32)
          + jnp.dot(ya_ref[...], wo_ref[D_HYENA:, :], preferred_element_type=F32))
    h2 = (_rms(x1) * gx_ref[...]).astype(BF16)
    qm = jnp.dot(h2, wq_ref[...], preferred_element_type=F32)
    heads = []
    for i in range(MEM_HEADS):
        sl = slice(i * MEM_HEAD_DIM, (i + 1) * MEM_HEAD_DIM)
        qn = (_rms(qm[:, sl]) * qg_ref[...] * (MEM_HEAD_DIM ** -0.5)).astype(BF16)
        s = lax.dot_general(qn, km_ref[0, :, sl], (((1,), (1,)), ((), ())), preferred_element_type=F32)
        p = jnp.exp(s - jnp.max(s, axis=-1, keepdims=True))
        o = jnp.dot(p.astype(BF16), vm_ref[0, :, sl], preferred_element_type=F32)
        heads.append(o / jnp.sum(p, axis=-1, keepdims=True))
    om = jnp.concatenate(heads, axis=1).astype(BF16)
    x2 = x1 + jnp.dot(om, wom_ref[...], preferred_element_type=F32)
    xw_ref[:, :D_MODEL] = x2

    h3 = _rms(x2) * gmoe_ref[...]
    h_hi, h_lo = _split_bf16(h3)
    dn = (((1,), (1,)), ((), ()))
    lt = (lax.dot_general(wrh_ref[...], h_hi, dn, preferred_element_type=F32)
          + lax.dot_general(wrh_ref[...], h_lo, dn, preferred_element_type=F32)
          + lax.dot_general(wrl_ref[...], h_hi, dn, preferred_element_type=F32)) + br_ref[...]
    g = [lt[i:i + 1, :] for i in range(N_GROUPS)]
    gmax = jnp.maximum(jnp.maximum(g[0], g[1]), jnp.maximum(g[2], g[3]))
    p_grp = 1.0 / sum(jnp.exp(gi - gmax) for gi in g)
    gidx = jnp.where(g[0] == gmax, 0, jnp.where(g[1] == gmax, 1, jnp.where(g[2] == gmax, 2, 3)))
    sel = []
    for j in range(EXPERTS_PER_GROUP):
        rows = [lt[N_GROUPS + EXPERTS_PER_GROUP * i + j:N_GROUPS + EXPERTS_PER_GROUP * i + j + 1, :]
                for i in range(N_GROUPS)]
        sel.append(jnp.where(gidx == 0, rows[0], jnp.where(gidx == 1, rows[1],
                                                            jnp.where(gidx == 2, rows[2], rows[3]))))
    emax = jnp.maximum(jnp.maximum(sel[0], sel[1]), jnp.maximum(sel[2], sel[3]))
    ee = [jnp.exp(sj - emax) for sj in sel]

    def first_argmax(vals):
        best = jnp.maximum(jnp.maximum(vals[0], vals[1]), jnp.maximum(vals[2], vals[3]))
        idx = jnp.where(vals[0] == best, 0, jnp.where(vals[1] == best, 1, jnp.where(vals[2] == best, 2, 3)))
        return idx, best

    a, pa = first_argmax(ee)
    b, pb = first_argmax([jnp.where(a == j, -1.0, ee[j]) for j in range(EXPERTS_PER_GROUP)])
    wa = p_grp * pa / (pa + pb)
    wb = p_grp * pb / (pa + pb)
    lo, hi = jnp.minimum(a, b), jnp.maximum(a, b)
    w_lo = jnp.where(a < b, wa, wb)
    w_hi = jnp.where(a < b, wb, wa)
    pair = jnp.where(lo == 0, hi - 1, jnp.where(lo == 1, hi + 1, 5))
    cls = gidx * len(PAIRS) + pair

    rows_i = lax.broadcasted_iota(jnp.int32, (ROUTE_ROWS, tm), 0)
    onehot = rows_i == cls
    prefix = jnp.dot(jnp.where(onehot, 1.0, 0.0).astype(BF16), tri_ref[...], preferred_element_type=F32)
    rank = jnp.sum(jnp.where(onehot, prefix, 0.0), axis=0, keepdims=True) - 1.0
    rr = lax.broadcasted_iota(jnp.int32, (8, tm), 0)
    route_ref[...] = jnp.where(rr == 0, cls.astype(F32), jnp.where(rr == 1, rank, 0.0))
    cnt_ref[0] = prefix[:, tm - LANE:]
    rw = lax.broadcasted_iota(jnp.int32, (LANE, tm), 0)
    wt = jnp.where(rw == 0, w_lo, jnp.where(rw == 1, w_hi, 0.0))
    xw_ref[:, D_MODEL:] = wt.T


def _mix(x, y_hy, y_at, gh, wo_bf, gx, wq_bf, qg, km, vm, wom_bf, gmoe, wr_hi, wr_lo, br, tm):
    B, L, D = x.shape
    nt = L // tm
    T = B * L
    M = km.shape[1]
    gmat = _group_sumsq_matrix(D_HYENA, D_HYENA // HYENA_GROUPS)
    tri = jnp.asarray(np.triu(np.ones((tm, tm), np.float32)), dtype=BF16)
    const = lambda shape: pl.BlockSpec(shape, lambda i: (0,) * len(shape))
    return pl.pallas_call(
        _mix_kernel,
        grid=(B * nt,),
        in_specs=[
            pl.BlockSpec((tm, D), lambda i: (i, 0)),
            pl.BlockSpec((tm, D_HYENA), lambda i: (i, 0)),
            pl.BlockSpec((tm, D_ATTN), lambda i: (i, 0)),
            const((1, D_HYENA)), const((D_HYENA, D_HYENA)), const((D_HYENA + D_ATTN, D)),
            const((1, D)), const((D, D_MEM_ATTN)), const((1, MEM_HEAD_DIM)),
            pl.BlockSpec((1, M, D_MEM_ATTN), lambda i: (i // nt, 0, 0)),
            pl.BlockSpec((1, M, D_MEM_ATTN), lambda i: (i // nt, 0, 0)),
            const((D_MEM_ATTN, D)), const((1, D)),
            const((ROUTE_ROWS, D)), const((ROUTE_ROWS, D)), const((ROUTE_ROWS, 1)),
            const((tm, tm)),
        ],
        out_specs=[
            pl.BlockSpec((tm, ROW_W), lambda i: (i, 0)),
            pl.BlockSpec((8, tm), lambda i: (0, i)),
            pl.BlockSpec((1, ROUTE_ROWS, LANE), lambda i: (i, 0, 0)),
        ],
        out_shape=[
            jax.ShapeDtypeStruct((T, ROW_W), F32),
            jax.ShapeDtypeStruct((8, T), F32),
            jax.ShapeDtypeStruct((B * nt, ROUTE_ROWS, LANE), F32),
        ],
        compiler_params=pltpu.CompilerParams(dimension_semantics=("parallel",),
                                             vmem_limit_bytes=VMEM_LIMIT),
        name="mix_mem_route",
    )(x.reshape(T, D), y_hy.reshape(T, D_HYENA), y_at.reshape(T, D_ATTN), gh, gmat, wo_bf, gx, wq_bf, qg,
      km, vm, wom_bf, gmoe, wr_hi, wr_lo, br, tri)


def _row_copy(src_ref, src_row, dst_ref, dst_row, n, sem):
    return pltpu.make_async_copy(src_ref.at[pl.ds(src_row, n)], dst_ref.at[pl.ds(dst_row, n)], sem)


def _scatter_rows_kernel(idx_ref, src_ref, init_ref, dst_ref, sem, *, rows):
    del init_ref
    base = pl.program_id(0) * rows

    def issue(r, carry):
        _row_copy(src_ref, base + r, dst_ref, idx_ref[0, 0, r], 1, sem).start()
        return carry

    lax.fori_loop(0, rows, issue, 0)
    _row_copy(src_ref, 0, dst_ref, 0, rows, sem).wait()


def _index_spec(rows):
    return pl.BlockSpec((1, 1, rows), lambda i: (i, 0, 0), memory_space=pltpu.SMEM)


def _scatter_rows(src, dest, n_out, rows):
    T, W = src.shape
    init = jnp.zeros((n_out, W), src.dtype)
    return pl.pallas_call(
        functools.partial(_scatter_rows_kernel, rows=rows),
        grid=(T // rows,),
        in_specs=[_index_spec(rows), pl.BlockSpec(memory_space=pl.ANY), pl.BlockSpec(memory_space=pl.ANY)],
        out_specs=pl.BlockSpec(memory_space=pl.ANY),
        scratch_shapes=[pltpu.SemaphoreType.DMA(())],
        out_shape=jax.ShapeDtypeStruct((n_out, W), src.dtype),
        input_output_aliases={2: 0},
        compiler_params=pltpu.CompilerParams(dimension_semantics=("arbitrary",)),
        name="scatter_rows",
    )(dest.reshape(T // rows, 1, rows), src, init)


def _gather_rows_kernel(idx_ref, src_ref, dst_ref, sem, *, rows):
    base = pl.program_id(0) * rows

    def issue(r, carry):
        _row_copy(src_ref, idx_ref[0, 0, r], dst_ref, base + r, 1, sem).start()
        return carry

    lax.fori_loop(0, rows, issue, 0)
    _row_copy(src_ref, 0, dst_ref, 0, rows, sem).wait()


def _gather_rows(src, idx, rows):
    T = idx.shape[0]
    W = src.shape[1]
    return pl.pallas_call(
        functools.partial(_gather_rows_kernel, rows=rows),
        grid=(T // rows,),
        in_specs=[_index_spec(rows), pl.BlockSpec(memory_space=pl.ANY)],
        out_specs=pl.BlockSpec(memory_space=pl.ANY),
        scratch_shapes=[pltpu.SemaphoreType.DMA(())],
        out_shape=jax.ShapeDtypeStruct((T, W), src.dtype),
        compiler_params=pltpu.CompilerParams(dimension_semantics=("arbitrary",)),
        name="gather_rows",
    )(idx.reshape(T // rows, 1, rows), src)


def _moe_kernel(lo_ref, hi_ref, valid_ref, xs_ref, g_ref, gu_lo_ref, dn_lo_ref, gu_hi_ref, dn_hi_ref, o_ref):
    del lo_ref, hi_ref
    i = pl.program_id(0)

    @pl.when(valid_ref[i] > 0)
    def _():
        x2 = xs_ref[:, :D_MODEL]
        h = (_rms(x2) * g_ref[...]).astype(BF16)
        y = x2
        for col, gu_ref, dn_ref in ((0, gu_lo_ref, dn_lo_ref), (1, gu_hi_ref, dn_hi_ref)):
            gu = jnp.dot(h, gu_ref[0], preferred_element_type=F32)
            act = jax.nn.silu(gu[:, :D_EXPERT]) * gu[:, D_EXPERT:]
            ye = jnp.dot(act.astype(BF16), dn_ref[0], preferred_element_type=F32)
            y = y + xs_ref[:, D_MODEL + col:D_MODEL + col + 1] * ye
        o_ref[...] = y

    @pl.when(valid_ref[i] == 0)
    def _():
        o_ref[...] = jnp.zeros_like(o_ref)


def _moe(xs, tile_lo, tile_hi, tile_valid, gmoe, w_gu_bf, w_dn_bf, ts):
    n_tiles = tile_lo.shape[0]
    D = D_MODEL
    return pl.pallas_call(
        _moe_kernel,
        grid_spec=pltpu.PrefetchScalarGridSpec(
            num_scalar_prefetch=3,
            grid=(n_tiles,),
            in_specs=[
                pl.BlockSpec((ts, ROW_W), lambda i, lo, hi, va: (i, 0)),
                pl.BlockSpec((1, D), lambda i, lo, hi, va: (0, 0)),
                pl.BlockSpec((1, D, 2 * D_EXPERT), lambda i, lo, hi, va: (lo[i], 0, 0)),
                pl.BlockSpec((1, D_EXPERT, D), lambda i, lo, hi, va: (lo[i], 0, 0)),
                pl.BlockSpec((1, D, 2 * D_EXPERT), lambda i, lo, hi, va: (hi[i], 0, 0)),
                pl.BlockSpec((1, D_EXPERT, D), lambda i, lo, hi, va: (hi[i], 0, 0)),
            ],
            out_specs=pl.BlockSpec((ts, D), lambda i, lo, hi, va: (i, 0)),
        ),
        out_shape=jax.ShapeDtypeStruct((n_tiles * ts, D), F32),
        compiler_params=pltpu.CompilerParams(dimension_semantics=("arbitrary",),
                                             vmem_limit_bytes=VMEM_LIMIT),
        name="moe_experts",
    )(tile_lo, tile_hi, tile_valid, xs, gmoe, w_gu_bf, w_dn_bf, w_gu_bf, w_dn_bf)


def _route_tables(route, counts, tm, ts):
    T = route.shape[1]
    cls = route[0].astype(jnp.int32)
    rank = route[1].astype(jnp.int32)
    cnt = counts[:, :N_CLASSES, LANE - 1].astype(jnp.int32)
    total = jnp.sum(cnt, axis=0)
    tiles_c = (total + ts - 1) // ts
    tile_start = jnp.cumsum(tiles_c) - tiles_c
    before = jnp.cumsum(cnt, axis=0) - cnt
    base = tile_start[None, :] * ts + before
    dest = jnp.take_along_axis(base, cls.reshape(-1, tm), axis=1).reshape(T) + rank
    n_tiles = T // ts + N_CLASSES
    tile_id = jnp.arange(n_tiles, dtype=jnp.int32)
    used = jnp.sum(tiles_c)
    tcls = jnp.clip(jnp.searchsorted(jnp.cumsum(tiles_c), tile_id, side="right"), 0, N_CLASSES - 1)
    last_cls = jnp.clip(jnp.searchsorted(jnp.cumsum(tiles_c), used - 1, side="right"), 0, N_CLASSES - 1)
    tcls = jnp.where(tile_id < used, tcls, last_cls).astype(jnp.int32)
    pair_lo = jnp.asarray([p[0] for p in PAIRS], jnp.int32)
    pair_hi = jnp.asarray([p[1] for p in PAIRS], jnp.int32)
    grp, pr = tcls // len(PAIRS), tcls % len(PAIRS)
    tile_lo = grp * EXPERTS_PER_GROUP + pair_lo[pr]
    tile_hi = grp * EXPERTS_PER_GROUP + pair_hi[pr]
    return dest, tile_lo, tile_hi, (tile_id < used).astype(jnp.int32)


def _hyena_filter_spectra(L, w1, b1, w2, b2, w3, sin_freq):
    t = jnp.linspace(0.0, 1.0, L, dtype=F32)[:, None]
    bands = jnp.linspace(1e-4, FILTER_BANDS - 1, FILTER_BANDS, dtype=F32)[None, :]
    w = (2.0 * math.pi / L) * jnp.arange(L, dtype=F32)[:, None]
    z = jnp.concatenate([t, jnp.cos(bands * w), jnp.sin(bands * w)], axis=-1)
    a = jnp.sin(sin_freq[0] * (z @ w1 + b1))
    a = jnp.sin(sin_freq[1] * (a @ w2 + b2))
    filt = (a @ w3).reshape(L, HYENA_ORDER, 2, D_HYENA)
    deltas = jnp.abs(jnp.linspace(math.log(DECAY_TARGET) / SLOW_DECAY_PCT,
                                  math.log(DECAY_TARGET) / FAST_DECAY_PCT, D_HYENA, dtype=F32))
    filt = filt * jnp.exp(-t * deltas)[:, None, None, :]
    fwd, bwd = filt[:, :, 0], filt[:, :, 1]
    k2 = jnp.concatenate([fwd, jnp.zeros_like(fwd[:1]), bwd[:0:-1]], axis=0)
    k2 = k2 / (jnp.sum(jnp.abs(k2), axis=0, keepdims=True) + EPS)
    return jnp.fft.rfft(k2, axis=0)


def _hyena(u, short_w, short_b, skip, k_f):
    B, L, C = u.shape
    up = jnp.pad(u, ((0, 0), (1, 1), (0, 0)))
    u = up[:, :-2] * short_w[0] + up[:, 1:-1] * short_w[1] + up[:, 2:] * short_w[2] + short_b
    v, g1, g2 = jnp.split(u, 3, axis=-1)
    z = v
    for o, gate in enumerate((g1, g2)):
        zf = jnp.fft.rfft(z, n=2 * L, axis=1)
        zc = jnp.fft.irfft(zf * k_f[None, :, o], n=2 * L, axis=1)[:, :L] + skip[o] * z
        z = gate * zc
    return z


def _pick_tile(n, pref):
    t = min(pref, n)
    while n % t:
        t //= 2
    return t


def kernel(x, mem, norm_mix, w_in, hyena_short_w, hyena_short_b, filt_w1, filt_b1, filt_w2, filt_b2, filt_w3, filt_sin_freq, hyena_skip, attn_q_norm, attn_k_norm, out_norm_hyena, out_norm_attn, w_out, norm_xattn, norm_mem, w_q_mem, w_k_mem, w_v_mem, mem_q_norm, mem_k_norm, w_o_mem, norm_moe, w_router_grp, b_router_grp, w_router_exp, b_router_exp, w_gate, w_up, w_down):
    B, L, D = x.shape
    T = B * L
    depth = norm_mix.shape[0]
    tm = _pick_tile(L, 512)
    tq = _pick_tile(L, 256)
    tk = _pick_tile(L, 512)
    ts = _pick_tile(T, 256)
    rows_per_step = _pick_tile(T, 2048)
    cos_t, sin_t = _rope_tables(L)
    for i in range(depth):
        qkg = jnp.concatenate([jnp.tile(attn_q_norm[i], N_Q_HEADS), jnp.tile(attn_k_norm[i], N_KV_HEADS)])[None, :]
        u, q, k, v = _inproj(x, norm_mix[i][None, :], w_in[i].astype(BF16), cos_t, sin_t, qkg, tm)
        k_f = _hyena_filter_spectra(L, filt_w1[i], filt_b1[i], filt_w2[i], filt_b2[i], filt_w3[i],
                                    filt_sin_freq[i])
        y_hy = _hyena(u.reshape(B, L, D_U), hyena_short_w[i], hyena_short_b[i], hyena_skip[i], k_f)
        y_at = _attention(q, k, v, out_norm_attn[i][None, :], tq, tk)
        km, vm = _memkv(mem, norm_mem[i][None, :], w_k_mem[i].astype(BF16), w_v_mem[i].astype(BF16),
                        mem_k_norm[i][None, :])
        wr = jnp.concatenate([w_router_grp[i], w_router_exp[i]], axis=1).T
        wr = jnp.pad(wr, ((0, ROUTE_ROWS - wr.shape[0]), (0, 0)))
        wr_hi, wr_lo = _split_bf16(wr)
        br = jnp.pad(jnp.concatenate([b_router_grp[i], b_router_exp[i]]), (0, ROUTE_ROWS - N_GROUPS - N_EXPERTS))
        xw, route, counts = _mix(x, y_hy, y_at, out_norm_hyena[i][None, :], w_out[i].astype(BF16),
                                 norm_xattn[i][None, :], w_q_mem[i].astype(BF16), mem_q_norm[i][None, :],
                                 km, vm, w_o_mem[i].astype(BF16), norm_moe[i][None, :], wr_hi, wr_lo,
                                 br[:, None], tm)
        dest, tile_lo, tile_hi, tile_valid = _route_tables(route, counts, tm, ts)
        n_tiles = T // ts + N_CLASSES
        xs = _scatter_rows(xw, dest, n_tiles * ts, rows_per_step)
        w_gu = jnp.concatenate([w_gate[i], w_up[i]], axis=-1).astype(BF16)
        ys = _moe(xs, tile_lo, tile_hi, tile_valid, norm_moe[i][None, :], w_gu, w_down[i].astype(BF16), ts)
        x = _gather_rows(ys, dest, rows_per_step).reshape(B, L, D)
    return x
```

```python
import functools
import math

import jax
import jax.numpy as jnp
import numpy as np
from jax import lax
from jax.experimental import pallas as pl
from jax.experimental.pallas import tpu as pltpu

F32 = jnp.float32
BF16 = jnp.bfloat16
U32 = jnp.uint32

D_MODEL = 1024
GRID_W = 64
D_HYENA = 512
HYENA_GROUPS = 8
HYENA_ORDER = 2
SHORT_CONV = 3
FILTER_BANDS = 16
FILTER_FF = 64
DECAY_TARGET = 1e-2
FAST_DECAY_PCT = 0.3
SLOW_DECAY_PCT = 1.5
N_Q_HEADS = 8
N_KV_HEADS = 2
HEAD_DIM = 64
D_ATTN = N_Q_HEADS * HEAD_DIM
D_KV = N_KV_HEADS * HEAD_DIM
ROPE_THETA = 10000.0
D_U = (HYENA_ORDER + 1) * D_HYENA
D_IN = D_U + D_ATTN + 2 * D_KV
D_QK = D_ATTN + D_KV
MEM_HEADS = 4
MEM_HEAD_DIM = 128
D_MEM_ATTN = MEM_HEADS * MEM_HEAD_DIM
N_GROUPS = 4
EXPERTS_PER_GROUP = 4
N_EXPERTS = N_GROUPS * EXPERTS_PER_GROUP
D_EXPERT = 512
EPS = 1e-6

PAIRS = ((0, 1), (0, 2), (0, 3), (1, 2), (1, 3), (2, 3))
N_CLASSES = N_GROUPS * len(PAIRS)
ROUTE_ROWS = 32
LANE = 128
SUBLANE = 8
SLAB = SUBLANE
VMEM_LIMIT = 56 * 1024 * 1024
HY_CB = 256
HY_PAD = 8
HI16 = np.uint32(0xFFFF0000)


def _rms(x):
    return x * lax.rsqrt(jnp.mean(x * x, axis=-1, keepdims=True) + EPS)


def _group_sumsq_matrix(width, group):
    idx = np.arange(width) // group
    return jnp.asarray((idx[:, None] == idx[None, :]).astype(np.float32) / group, dtype=BF16)


def _pack_bf16_pair(a, b):
    ha = lax.bitcast_convert_type(a.astype(BF16).astype(F32), U32)
    hb = lax.bitcast_convert_type(b.astype(BF16).astype(F32), U32)
    return ha | (hb >> 16)


def _unpack_hi(w):
    return lax.bitcast_convert_type(w & HI16, F32)


def _unpack_lo(w):
    return lax.bitcast_convert_type(w << 16, F32)


def _tile_rows(x_ref, ct, d):
    return jnp.concatenate([x_ref[0, :, c * d:(c + 1) * d] for c in range(ct)], axis=0)


def _inproj_kernel(x_ref, g_ref, w_ref, cos_ref, sin_ref, qkg_ref, gmat_ref,
                   u_ref, q_ref, k_ref, v_ref, *, ct):
    x = _tile_rows(x_ref, ct, D_MODEL)
    tm = x.shape[0]
    h = _rms(x) * g_ref[...]
    p = jnp.dot(h.astype(BF16), w_ref[...], preferred_element_type=F32)
    u_ref[...] = p[:, :D_U].astype(BF16)
    qk = p[:, D_U:D_U + D_QK]
    ms = jnp.dot((qk * qk).astype(BF16), gmat_ref[...], preferred_element_type=F32)
    qk = qk * lax.rsqrt(ms + EPS) * qkg_ref[...]
    reps = D_QK // LANE
    cos = jnp.concatenate([cos_ref[...]] * reps, axis=1)
    sin = jnp.concatenate([sin_ref[...]] * reps, axis=1)
    lane = lax.broadcasted_iota(jnp.int32, (tm, D_QK), 1)
    partner = jnp.where((lane % 32) < 16, pltpu.roll(qk, D_QK - 16, 1), pltpu.roll(qk, 16, 1))
    qk = qk * cos + partner * sin
    for hd in range(N_Q_HEADS):
        q_ref[0, hd] = (qk[:, hd * HEAD_DIM:(hd + 1) * HEAD_DIM] * (HEAD_DIM ** -0.5)).astype(BF16)
    for hd in range(N_KV_HEADS):
        k_ref[0, hd] = qk[:, D_ATTN + hd * HEAD_DIM:D_ATTN + (hd + 1) * HEAD_DIM].astype(BF16)
    vv = p[:, D_U + D_QK:]
    lane_v = lax.broadcasted_iota(jnp.int32, (tm, LANE), 1)
    ones_col = jnp.where(lane_v == HEAD_DIM, 1.0, 0.0)
    v_ref[0, 0] = jnp.where(lane_v < HEAD_DIM, vv, ones_col).astype(BF16)
    v_ref[0, 1] = jnp.where(lane_v < HEAD_DIM, pltpu.roll(vv, HEAD_DIM, 1), ones_col).astype(BF16)


def _inproj(xv, g, w_bf, cos_t, sin_t, qkg, ct):
    B, NR, _ = xv.shape
    D = D_MODEL
    L = NR * GRID_W
    tm = ct * NR
    nt = L // tm
    gmat = _group_sumsq_matrix(D_QK, HEAD_DIM)
    return pl.pallas_call(
        functools.partial(_inproj_kernel, ct=ct),
        grid=(B * nt,),
        in_specs=[
            pl.BlockSpec((1, NR, ct * D), lambda i: (i // nt, 0, i % nt)),
            pl.BlockSpec((1, D), lambda i: (0, 0)),
            pl.BlockSpec((D, D_IN), lambda i: (0, 0)),
            pl.BlockSpec((tm, LANE), lambda i: (i % nt, 0)),
            pl.BlockSpec((tm, LANE), lambda i: (i % nt, 0)),
            pl.BlockSpec((1, D_QK), lambda i: (0, 0)),
            pl.BlockSpec((D_QK, D_QK), lambda i: (0, 0)),
        ],
        out_specs=[
            pl.BlockSpec((tm, D_U), lambda i: (i, 0)),
            pl.BlockSpec((1, N_Q_HEADS, tm, HEAD_DIM), lambda i: (i // nt, 0, i % nt, 0)),
            pl.BlockSpec((1, N_KV_HEADS, tm, HEAD_DIM), lambda i: (i // nt, 0, i % nt, 0)),
            pl.BlockSpec((1, N_KV_HEADS, tm, LANE), lambda i: (i // nt, 0, i % nt, 0)),
        ],
        out_shape=[
            jax.ShapeDtypeStruct((B * L, D_U), BF16),
            jax.ShapeDtypeStruct((B, N_Q_HEADS, L, HEAD_DIM), BF16),
            jax.ShapeDtypeStruct((B, N_KV_HEADS, L, HEAD_DIM), BF16),
            jax.ShapeDtypeStruct((B, N_KV_HEADS, L, LANE), BF16),
        ],
        compiler_params=pltpu.CompilerParams(dimension_semantics=("parallel",),
                                             vmem_limit_bytes=VMEM_LIMIT),
        name="inproj",
    )(xv, g, w_bf, cos_t, sin_t, qkg, gmat)


def _rope_tables(L):
    NR = L // GRID_W
    p = np.arange(L)
    row, col = p % NR, p // NR
    d = np.arange(LANE) % HEAD_DIM
    sec, r = d // 32, d % 32
    inv = (ROPE_THETA ** (-(r % 16).astype(np.float64) / 16.0)).astype(np.float32)
    pos = np.where(sec[None, :] == 0, row[:, None], col[:, None]).astype(np.float32)
    ang = pos * inv[None, :]
    sign = np.where(r < 16, -1.0, 1.0)[None, :]
    return jnp.asarray(np.cos(ang), F32), jnp.asarray(np.sin(ang) * sign, F32)


def _real_embed(m):
    return np.block([[m.real, -m.imag], [m.imag, m.real]])


@functools.lru_cache(maxsize=None)
def _dft_tables(n1, n2):
    L = n1 * n2
    i1, i2 = np.arange(n1), np.arange(n2)
    om = np.exp(-1j * np.pi / L)
    f1 = np.exp(-2j * np.pi * np.outer(i1, i1) / n1)
    tw = np.exp(-2j * np.pi * np.outer(i1, i2) / L)
    fwd1, fwd1_real = [], []
    for t in i2:
        ev = tw[:, t][:, None] * f1
        od = (om ** t) * tw[:, t][:, None] * f1 * (om ** (n2 * i1))[None, :]
        fwd1.append(np.concatenate([_real_embed(ev), _real_embed(od)], axis=0))
        z = np.zeros((n1, n1))
        fwd1_real.append(np.block([[ev.real, z], [ev.imag, z], [z, od.real], [z, od.imag]]))
    f2 = _real_embed(np.exp(-2j * np.pi * np.outer(i2, i2) / n2))
    f2i = np.exp(2j * np.pi * np.outer(i2, i2) / n2)
    inv_e = np.stack([_real_embed(np.conj(tw[k, :])[:, None] * f2i) for k in i1])
    inv_o = np.stack([_real_embed((om ** (-i2))[:, None] * np.conj(tw[k, :])[:, None] * f2i) for k in i1])
    f1i = np.exp(2j * np.pi * np.outer(i1, i1) / n1)
    inv2 = np.concatenate([_real_embed(f1i), _real_embed((om ** (-n2 * i1))[:, None] * f1i)], axis=1) / (2 * L)
    as_bf = lambda a: jnp.asarray(a, dtype=BF16)
    return {"fwd1": as_bf(np.stack(fwd1)), "fwd1_real": as_bf(np.stack(fwd1_real)), "fwd2": as_bf(f2),
            "inv_e": as_bf(inv_e), "inv_o": as_bf(inv_o), "inv2": as_bf(inv2)}


def _conv_block(ref, w_ref, b_ref, base, t2, n1, n2):
    def blk(i):
        return ref[pl.ds(pl.multiple_of(base + i * n1, n1), n1), :].astype(F32)

    cur, prev, nxt = blk(t2), blk((t2 + n2 - 1) % n2), blk((t2 + 1) % n2)
    row = lax.broadcasted_iota(jnp.int32, cur.shape, 0)
    prev_wrapped = jnp.where(row == 0, 0.0, pltpu.roll(prev, 1, 0))
    nxt_wrapped = jnp.where(row == n1 - 1, 0.0, pltpu.roll(nxt, n1 - 1, 0))
    prev = jnp.where(t2 == 0, prev_wrapped, prev)
    nxt = jnp.where(t2 == n2 - 1, nxt_wrapped, nxt)
    return prev * w_ref[0:1, :] + cur * w_ref[1:2, :] + nxt * w_ref[2:3, :] + b_ref[...]


def _plain_block(ref, base, t2, n1):
    return ref[pl.ds(pl.multiple_of(base + t2 * n1, n1), n1), :].astype(F32)


def _store_slabs(a_ref, rows, val):
    for s in range(val.shape[1] // LANE):
        a_ref[s, rows, :] = val[:, s * LANE:(s + 1) * LANE]


def _load_slabs(a_ref, rows):
    return jnp.concatenate([a_ref[s, rows, :] for s in range(a_ref.shape[0])], axis=1)


def _forward_stage1(load_stacked, fwd1_ref, a_ref, n1, n2):
    pitch = 2 * n1 + HY_PAD

    def body(t2, carry):
        a = jnp.dot(fwd1_ref[t2], load_stacked(t2).astype(BF16), preferred_element_type=F32)
        base = t2 * pitch
        _store_slabs(a_ref, pl.ds(base, n1), _pack_bf16_pair(a[:n1], a[n1:2 * n1]))
        _store_slabs(a_ref, pl.ds(base + n1, n1), _pack_bf16_pair(a[2 * n1:3 * n1], a[3 * n1:]))
        return carry

    lax.fori_loop(0, n2, body, 0)


def _forward_stage2(a_ref, fwd2_ref, k1, n1, n2):
    pitch = 2 * n1 + HY_PAD
    we = _load_slabs(a_ref, pl.ds(k1, n2, stride=pitch))
    wo = _load_slabs(a_ref, pl.ds(n1 + k1, n2, stride=pitch))
    xin = jnp.concatenate([jnp.concatenate([_unpack_hi(we), _unpack_lo(we)], axis=0),
                           jnp.concatenate([_unpack_hi(wo), _unpack_lo(wo)], axis=0)], axis=1)
    x = jnp.dot(fwd2_ref[...], xin.astype(BF16), preferred_element_type=F32)
    return x[:n2], x[n2:]


def _hyena_kernel(*refs, n1, n2, conv_z):
    L = n1 * n2
    if conv_z:
        (z_ref, zw_ref, zb_ref, g_ref, gw_ref, gb_ref, ker_ref, kei_ref, kor_ref, koi_ref,
         skip_ref, fwd1_ref, fwd2_ref, inve_ref, invo_ref, inv2_ref, o_ref, a_ref) = refs
        load_z = lambda part, t2: _conv_block(z_ref, zw_ref, zb_ref, part * L, t2, n1, n2)
    else:
        (z_ref, g_ref, gw_ref, gb_ref, ker_ref, kei_ref, kor_ref, koi_ref,
         skip_ref, fwd1_ref, fwd2_ref, inve_ref, invo_ref, inv2_ref, o_ref, a_ref) = refs
        load_z = lambda part, t2: _plain_block(z_ref, part * L, t2, n1)
    cb = z_ref.shape[1]
    pitch = 2 * n1 + HY_PAD

    _forward_stage1(lambda t2: jnp.concatenate([load_z(0, t2), load_z(1, t2)], axis=0),
                    fwd1_ref, a_ref, n1, n2)

    def freq_body(k1, carry):
        xr, xi = _forward_stage2(a_ref, fwd2_ref, k1, n1, n2)
        rows = pl.ds(pl.multiple_of(k1 * n2, n2), n2)
        kr = jnp.concatenate([ker_ref[rows, :], kor_ref[rows, :]], axis=1).astype(F32)
        ki = jnp.concatenate([kei_ref[rows, :], koi_ref[rows, :]], axis=1).astype(F32)
        yr = xr * kr - xi * ki
        yi = xr * ki + xi * kr
        ye = jnp.concatenate([yr[:, :cb], yi[:, :cb]], axis=0).astype(BF16)
        yo = jnp.concatenate([yr[:, cb:], yi[:, cb:]], axis=0).astype(BF16)
        be = jnp.dot(inve_ref[k1], ye, preferred_element_type=F32)
        bo = jnp.dot(invo_ref[k1], yo, preferred_element_type=F32)
        _store_slabs(a_ref, pl.ds(k1, n2, stride=pitch), _pack_bf16_pair(be[:n2], be[n2:]))
        _store_slabs(a_ref, pl.ds(n1 + k1, n2, stride=pitch), _pack_bf16_pair(bo[:n2], bo[n2:]))
        return carry

    lax.fori_loop(0, n1, freq_body, 0)

    def time_body(t2, carry):
        blk = _load_slabs(a_ref, pl.ds(t2 * pitch, 2 * n1))
        be, bo = blk[:n1], blk[n1:]
        bin_ = jnp.concatenate([_unpack_hi(be), _unpack_lo(be), _unpack_hi(bo), _unpack_lo(bo)], axis=0)
        y = jnp.dot(inv2_ref[...], bin_.astype(BF16), preferred_element_type=F32)
        for part in range(2):
            gate = _conv_block(g_ref, gw_ref, gb_ref, part * L, t2, n1, n2)
            zc = y[part * n1:(part + 1) * n1] + skip_ref[...] * load_z(part, t2)
            o_ref[pl.ds(pl.multiple_of(part * L + t2 * n1, n1), n1), :] = (gate * zc).astype(o_ref.dtype)
        return carry

    lax.fori_loop(0, n2, time_body, 0)


def _hyena_order(z, z_col0, u, gate_col0, short_w, short_b, spectra, spec_col0, skip, tabs, B, L, conv_z):
    n2 = GRID_W
    n1 = L // n2
    cb = HY_CB
    ncb = D_HYENA // cb
    pitch = 2 * n1 + HY_PAD
    col = lambda off: (lambda c, p: (0, off + c))
    pair = lambda off: (lambda c, p: (p, off + c))
    one = pl.Buffered(1)
    const3 = lambda a: pl.BlockSpec(a.shape, lambda c, p: (0, 0, 0))
    const2 = lambda a: pl.BlockSpec(a.shape, lambda c, p: (0, 0))
    in_specs = [pl.BlockSpec((2 * L, cb), pair(z_col0), pipeline_mode=one)]
    args = [z]
    if conv_z:
        in_specs += [pl.BlockSpec((SHORT_CONV, cb), col(z_col0)), pl.BlockSpec((1, cb), col(z_col0))]
        args += [short_w, short_b]
    in_specs += [pl.BlockSpec((2 * L, cb), pair(gate_col0), pipeline_mode=one),
                 pl.BlockSpec((SHORT_CONV, cb), col(gate_col0)), pl.BlockSpec((1, cb), col(gate_col0))]
    args += [u, short_w, short_b]
    in_specs += [pl.BlockSpec((L, cb), col(spec_col0), pipeline_mode=one)] * 4
    args += list(spectra)
    in_specs += [pl.BlockSpec((1, cb), lambda c, p: (0, c)),
                 const3(tabs["fwd1"]), const2(tabs["fwd2"]), const3(tabs["inv_e"]), const3(tabs["inv_o"]),
                 const2(tabs["inv2"])]
    args += [skip, tabs["fwd1"], tabs["fwd2"], tabs["inv_e"], tabs["inv_o"], tabs["inv2"]]
    return pl.pallas_call(
        functools.partial(_hyena_kernel, n1=n1, n2=n2, conv_z=conv_z),
        grid=(ncb, B // 2),
        in_specs=in_specs,
        out_specs=pl.BlockSpec((2 * L, cb), lambda c, p: (p, c)),
        out_shape=jax.ShapeDtypeStruct((B * L, D_HYENA), BF16),
        scratch_shapes=[pltpu.VMEM((cb // LANE, n2 * pitch, LANE), U32)],
        compiler_params=pltpu.CompilerParams(dimension_semantics=("arbitrary", "arbitrary"),
                                             vmem_limit_bytes=VMEM_LIMIT),
        name="hyena_order",
    )(*args)


def _spectra_kernel(kc_ref, kn_ref, fwd1_ref, fwd2_ref, er_ref, ei_ref, or_ref, oi_ref, a_ref, *, n1, n2):
    cb = kc_ref.shape[1]
    _forward_stage1(lambda t2: jnp.concatenate([_plain_block(kc_ref, 0, t2, n1), _plain_block(kn_ref, 0, t2, n1)],
                                               axis=0), fwd1_ref, a_ref, n1, n2)

    def freq_body(k1, carry):
        xr, xi = _forward_stage2(a_ref, fwd2_ref, k1, n1, n2)
        rows = pl.ds(pl.multiple_of(k1 * n2, n2), n2)
        er_ref[rows, :] = xr[:, :cb].astype(BF16)
        ei_ref[rows, :] = xi[:, :cb].astype(BF16)
        or_ref[rows, :] = xr[:, cb:].astype(BF16)
        oi_ref[rows, :] = xi[:, cb:].astype(BF16)
        return carry

    lax.fori_loop(0, n1, freq_body, 0)


def _filter_spectra(kc, kn, tabs, L):
    n2 = GRID_W
    n1 = L // n2
    cb = HY_CB
    C = kc.shape[1]
    pitch = 2 * n1 + HY_PAD
    return pl.pallas_call(
        functools.partial(_spectra_kernel, n1=n1, n2=n2),
        grid=(C // cb,),
        in_specs=[pl.BlockSpec((L, cb), lambda c: (0, c)), pl.BlockSpec((L, cb), lambda c: (0, c)),
                  pl.BlockSpec(tabs["fwd1_real"].shape, lambda c: (0, 0, 0)),
                  pl.BlockSpec(tabs["fwd2"].shape, lambda c: (0, 0))],
        out_specs=[pl.BlockSpec((L, cb), lambda c: (0, c))] * 4,
        out_shape=[jax.ShapeDtypeStruct((L, C), BF16)] * 4,
        scratch_shapes=[pltpu.VMEM((cb // LANE, n2 * pitch, LANE), U32)],
        compiler_params=pltpu.CompilerParams(dimension_semantics=("arbitrary",),
                                             vmem_limit_bytes=VMEM_LIMIT),
        name="filter_spectra",
    )(kc, kn, tabs["fwd1_real"], tabs["fwd2"])


def _filter_time_kernel(zf_ref, zr_ref, w1_ref, b1_ref, w2_ref, b2_ref, w3f_ref, w3b_ref, sf_ref,
                        df_ref, dr_ref, kc_ref, kn_ref):
    hp = lax.Precision.HIGHEST

    def ffn(z_ref, w3_ref):
        a = jnp.sin(sf_ref[0:1, :] * (jnp.dot(z_ref[...], w1_ref[...], precision=hp,
                                              preferred_element_type=F32) + b1_ref[...]))
        a = jnp.sin(sf_ref[1:2, :] * (jnp.dot(a, w2_ref[...], precision=hp,
                                              preferred_element_type=F32) + b2_ref[...]))
        return jnp.dot(a, w3_ref[...], precision=hp, preferred_element_type=F32)

    fwd = ffn(zf_ref, w3f_ref) * df_ref[...]
    bwd = ffn(zr_ref, w3b_ref) * dr_ref[...]
    norm = (jnp.sum(jnp.abs(fwd), axis=0, keepdims=True) + jnp.sum(jnp.abs(bwd), axis=0, keepdims=True)
            + EPS)
    kc_ref[...] = (fwd + bwd) / norm
    kn_ref[...] = (fwd - bwd) / norm


def _filter_time(w1, b1, w2, b2, w3, sin_freq, L):
    NR = L // GRID_W
    f64 = np.float64
    p = np.arange(L)
    t_nat = (p % NR) * GRID_W + p // NR
    tl = np.linspace(0.0, 1.0, L)
    bands = np.linspace(1e-4, FILTER_BANDS - 1, FILTER_BANDS)
    wv = (2.0 * math.pi / L) * np.arange(L, dtype=f64)
    feats = np.concatenate([tl[:, None], np.cos(bands[None, :] * wv[:, None]),
                            np.sin(bands[None, :] * wv[:, None])], axis=1)
    deltas = np.abs(np.linspace(math.log(DECAY_TARGET) / SLOW_DECAY_PCT,
                                math.log(DECAY_TARGET) / FAST_DECAY_PCT, D_HYENA))
    decay = np.exp(-tl[:, None] * deltas[None, :])
    t_rev = (L - t_nat) % L
    valid = (t_nat > 0).astype(f64)[:, None]
    pad = LANE - feats.shape[1]
    zf = np.pad(feats[t_nat], ((0, 0), (0, pad)))
    zr = np.pad(feats[t_rev], ((0, 0), (0, pad)))
    df = np.tile(decay[t_nat], (1, HYENA_ORDER))
    dr = np.tile(decay[t_rev] * valid, (1, HYENA_ORDER))
    w1p = jnp.pad(w1, ((0, pad), (0, 0)))
    w3r = w3.reshape(FILTER_FF, HYENA_ORDER, 2, D_HYENA)
    w3f = w3r[:, :, 0].reshape(FILTER_FF, HYENA_ORDER * D_HYENA)
    w3b = w3r[:, :, 1].reshape(FILTER_FF, HYENA_ORDER * D_HYENA)
    C = HYENA_ORDER * D_HYENA
    cb = HY_CB
    full = lambda a: pl.BlockSpec(a.shape, lambda c: (0,) * a.ndim)
    colb = lambda r: pl.BlockSpec((r, cb), lambda c: (0, c))
    zf, zr, df, dr = (jnp.asarray(a, F32) for a in (zf, zr, df, dr))
    args = (zf, zr, w1p, b1[None, :], w2, b2[None, :], w3f, w3b, sin_freq, df, dr)
    return pl.pallas_call(
        _filter_time_kernel,
        grid=(C // cb,),
        in_specs=[full(zf), full(zr), full(w1p), full(args[3]), full(w2), full(args[5]),
                  colb(FILTER_FF), colb(FILTER_FF), full(sin_freq), colb(L), colb(L)],
        out_specs=[colb(L), colb(L)],
        out_shape=[jax.ShapeDtypeStruct((L, C), F32)] * 2,
        compiler_params=pltpu.CompilerParams(dimension_semantics=("parallel",),
                                             vmem_limit_bytes=VMEM_LIMIT),
        name="filter_time",
    )(*args)


def _hyena(u, short_w, short_b, skip, w1, b1, w2, b2, w3, sin_freq, B, L):
    tabs = _dft_tables(L // GRID_W, GRID_W)
    kc, kn = _filter_time(w1, b1, w2, b2, w3, sin_freq, L)
    spectra = _filter_spectra(kc, kn, tabs, L)
    ncb = D_HYENA // HY_CB
    sb = short_b[None, :]
    z1 = _hyena_order(u, 0, u, ncb, short_w, sb, spectra, 0, skip[0:1], tabs, B, L, True)
    return _hyena_order(z1, 0, u, 2 * ncb, short_w, sb, spectra, ncb, skip[1:2], tabs, B, L, False)


def _attn_kernel(q_ref, k_ref, v_ref, g_ref, o_ref, *, tk):
    G, tq = q_ref.shape[1], q_ref.shape[2]
    L = k_ref.shape[2]
    R = G * tq
    q = q_ref[0].reshape(R, HEAD_DIM)
    m = jnp.full((R, 1), -jnp.inf, F32)
    acc = jnp.zeros((R, LANE), F32)
    for c in range(L // tk):
        kc = k_ref[0, 0, c * tk:(c + 1) * tk, :]
        s = lax.dot_general(q, kc, (((1,), (1,)), ((), ())), preferred_element_type=F32)
        m_new = jnp.maximum(m, jnp.max(s, axis=-1, keepdims=True))
        p = jnp.exp(s - m_new)
        acc = jnp.exp(m - m_new) * acc + jnp.dot(p.astype(BF16), v_ref[0, 0, c * tk:(c + 1) * tk, :],
                                                  preferred_element_type=F32)
        m = m_new
    o = acc[:, :HEAD_DIM] / acc[:, HEAD_DIM:HEAD_DIM + 1]
    o = _rms(o)
    o = jnp.concatenate([o[h * tq:(h + 1) * tq] for h in range(G)], axis=1)
    o_ref[0] = (o * g_ref[...]).astype(o_ref.dtype)


def _attention(q, k, v, g_out, tq, tk):
    B, H, L, _ = q.shape
    G = H // N_KV_HEADS
    return pl.pallas_call(
        functools.partial(_attn_kernel, tk=tk),
        grid=(B, N_KV_HEADS, L // tq),
        in_specs=[
            pl.BlockSpec((1, G, tq, HEAD_DIM), lambda b, kv, i: (b, kv, i, 0)),
            pl.BlockSpec((1, 1, L, HEAD_DIM), lambda b, kv, i: (b, kv, 0, 0)),
            pl.BlockSpec((1, 1, L, LANE), lambda b, kv, i: (b, kv, 0, 0)),
            pl.BlockSpec((1, G * HEAD_DIM), lambda b, kv, i: (0, kv)),
        ],
        out_specs=pl.BlockSpec((1, tq, G * HEAD_DIM), lambda b, kv, i: (b, i, kv)),
        out_shape=jax.ShapeDtypeStruct((B, L, D_ATTN), BF16),
        compiler_params=pltpu.CompilerParams(dimension_semantics=("parallel", "parallel", "parallel"),
                                             vmem_limit_bytes=VMEM_LIMIT),
        name="attention",
    )(q, k, v, g_out)


def _memkv_kernel(m_ref, g_ref, wk_ref, wv_ref, kg_ref, k_ref, v_ref):
    h = (_rms(m_ref[0]) * g_ref[...]).astype(BF16)
    k = jnp.dot(h, wk_ref[...], preferred_element_type=F32)
    v = jnp.dot(h, wv_ref[...], preferred_element_type=F32)
    kn = [_rms(k[:, i * MEM_HEAD_DIM:(i + 1) * MEM_HEAD_DIM]) * kg_ref[...] for i in range(MEM_HEADS)]
    k_ref[0] = jnp.concatenate(kn, axis=1).astype(BF16)
    v_ref[0] = v.astype(BF16)


def _memkv(mem, g, wk_bf, wv_bf, kg):
    B, M, D = mem.shape
    return pl.pallas_call(
        _memkv_kernel,
        grid=(B,),
        in_specs=[
            pl.BlockSpec((1, M, D), lambda b: (b, 0, 0)),
            pl.BlockSpec((1, D), lambda b: (0, 0)),
            pl.BlockSpec((D, D_MEM_ATTN), lambda b: (0, 0)),
            pl.BlockSpec((D, D_MEM_ATTN), lambda b: (0, 0)),
            pl.BlockSpec((1, MEM_HEAD_DIM), lambda b: (0, 0)),
        ],
        out_specs=[pl.BlockSpec((1, M, D_MEM_ATTN), lambda b: (b, 0, 0))] * 2,
        out_shape=[jax.ShapeDtypeStruct((B, M, D_MEM_ATTN), BF16)] * 2,
        compiler_params=pltpu.CompilerParams(dimension_semantics=("parallel",),
                                             vmem_limit_bytes=VMEM_LIMIT),
        name="memkv",
    )(mem, g, wk_bf, wv_bf, kg)


def _split_bf16(a):
    hi = a.astype(BF16)
    return hi, (a - hi.astype(F32)).astype(BF16)


def _mix_kernel(x_ref, yh_ref, ya_ref, gh_ref, gmat_ref, wo_ref, gx_ref, wq_ref, qg_ref,
                km_ref, vm_ref, wom_ref, gmoe_ref, wrh_ref, wrl_ref, br_ref, tri_ref,
                x2_ref, pay_ref, route_ref, cnt_ref, *, ct):
    x = _tile_rows(x_ref, ct, D_MODEL)
    tm = x.shape[0]
    yh = yh_ref[...].astype(F32)
    ms = jnp.dot((yh * yh).astype(BF16), gmat_ref[...], preferred_element_type=F32)
    yh = yh * lax.rsqrt(ms + EPS) * gh_ref[...]
    x1 = (x
          + jnp.dot(yh.astype(BF16), wo_ref[:D_HYENA, :], preferred_element_type=F32)
          + jnp.dot(ya_ref[...], wo_ref[D_HYENA:, :], preferred_element_type=F32))
    h2 = (_rms(x1) * gx_ref[...]).astype(BF16)
    qm = jnp.dot(h2, wq_ref[...], preferred_element_type=F32)
    heads = []
    for i in range(MEM_HEADS):
        sl = slice(i * MEM_HEAD_DIM, (i + 1) * MEM_HEAD_DIM)
        qn = (_rms(qm[:, sl]) * qg_ref[...] * (MEM_HEAD_DIM ** -0.5)).astype(BF16)
        s = lax.dot_general(qn, km_ref[0, :, sl], (((1,), (1,)), ((), ())), preferred_element_type=F32)
        p = jnp.exp(s - jnp.max(s, axis=-1, keepdims=True))
        o = jnp.dot(p.astype(BF16), vm_ref[0, :, sl], preferred_element_type=F32)
        heads.append(o / jnp.sum(p, axis=-1, keepdims=True))
    om = jnp.concatenate(heads, axis=1).astype(BF16)
    x2 = x1 + jnp.dot(om, wom_ref[...], preferred_element_type=F32)
    x2_ref[...] = x2

    h3 = _rms(x2) * gmoe_ref[...]
    h_hi, h_lo = _split_bf16(h3)
    dn = (((1,), (1,)), ((), ()))
    lt = (lax.dot_general(wrh_ref[...], h_hi, dn, preferred_element_type=F32)
          + lax.dot_general(wrh_ref[...], h_lo, dn, preferred_element_type=F32)
          + lax.dot_general(wrl_ref[...], h_hi, dn, preferred_element_type=F32)) + br_ref[...]
    g = [lt[i:i + 1, :] for i in range(N_GROUPS)]
    gmax = jnp.maximum(jnp.maximum(g[0], g[1]), jnp.maximum(g[2], g[3]))
    p_grp = 1.0 / sum(jnp.exp(gi - gmax) for gi in g)
    gidx = jnp.where(g[0] == gmax, 0, jnp.where(g[1] == gmax, 1, jnp.where(g[2] == gmax, 2, 3)))
    sel = []
    for j in range(EXPERTS_PER_GROUP):
        rows = [lt[N_GROUPS + EXPERTS_PER_GROUP * i + j:N_GROUPS + EXPERTS_PER_GROUP * i + j + 1, :]
                for i in range(N_GROUPS)]
        sel.append(jnp.where(gidx == 0, rows[0], jnp.where(gidx == 1, rows[1],
                                                            jnp.where(gidx == 2, rows[2], rows[3]))))
    emax = jnp.maximum(jnp.maximum(sel[0], sel[1]), jnp.maximum(sel[2], sel[3]))
    ee = [jnp.exp(sj - emax) for sj in sel]

    def first_argmax(vals):
        best = jnp.maximum(jnp.maximum(vals[0], vals[1]), jnp.maximum(vals[2], vals[3]))
        idx = jnp.where(vals[0] == best, 0, jnp.where(vals[1] == best, 1, jnp.where(vals[2] == best, 2, 3)))
        return idx, best

    a, pa = first_argmax(ee)
    b, pb = first_argmax([jnp.where(a == j, -1.0, ee[j]) for j in range(EXPERTS_PER_GROUP)])
    wa = p_grp * pa / (pa + pb)
    wb = p_grp * pb / (pa + pb)
    lo, hi = jnp.minimum(a, b), jnp.maximum(a, b)
    w_lo = jnp.where(a < b, wa, wb)
    w_hi = jnp.where(a < b, wb, wa)
    pair = jnp.where(lo == 0, hi - 1, jnp.where(lo == 1, hi + 1, 5))
    cls = gidx * len(PAIRS) + pair

    rows_i = lax.broadcasted_iota(jnp.int32, (ROUTE_ROWS, tm), 0)
    onehot = rows_i == cls
    prefix = jnp.dot(jnp.where(onehot, 1.0, 0.0).astype(BF16), tri_ref[...], preferred_element_type=F32)
    rank = jnp.sum(jnp.where(onehot, prefix, 0.0), axis=0, keepdims=True) - 1.0
    rr = lax.broadcasted_iota(jnp.int32, (8, tm), 0)
    route_ref[...] = jnp.where(rr == 0, cls.astype(F32), jnp.where(rr == 1, rank, 0.0))
    cnt_ref[0] = prefix[:, tm - LANE:]

    for j in range(D_MODEL // (2 * LANE)):
        pay_ref[pl.ds(j, tm, stride=SLAB), :] = _pack_bf16_pair(h3[:, 2 * j * LANE:(2 * j + 1) * LANE],
                                                                 h3[:, (2 * j + 1) * LANE:(2 * j + 2) * LANE])
    rw = lax.broadcasted_iota(jnp.int32, (LANE, tm), 0)
    wt = jnp.where(rw == 0, w_lo, jnp.where(rw == 1, w_hi, 0.0))
    pay_ref[pl.ds(4, tm, stride=SLAB), :] = lax.bitcast_convert_type(wt.T, U32)
    for j in range(5, SLAB):
        pay_ref[pl.ds(j, tm, stride=SLAB), :] = jnp.zeros((tm, LANE), U32)


def _mix(xv, y_hy, y_at, gh, wo_bf, gx, wq_bf, qg, km, vm, wom_bf, gmoe, wr_hi, wr_lo, br, ct):
    B, NR, _ = xv.shape
    D = D_MODEL
    L = NR * GRID_W
    tm = ct * NR
    nt = L // tm
    T = B * L
    M = km.shape[1]
    gmat = _group_sumsq_matrix(D_HYENA, D_HYENA // HYENA_GROUPS)
    tri = jnp.asarray(np.triu(np.ones((tm, tm), np.float32)), dtype=BF16)
    const = lambda shape: pl.BlockSpec(shape, lambda i: (0,) * len(shape))
    return pl.pallas_call(
        functools.partial(_mix_kernel, ct=ct),
        grid=(B * nt,),
        in_specs=[
            pl.BlockSpec((1, NR, ct * D), lambda i: (i // nt, 0, i % nt)),
            pl.BlockSpec((tm, D_HYENA), lambda i: (i, 0)),
            pl.BlockSpec((tm, D_ATTN), lambda i: (i, 0)),
            const((1, D_HYENA)), const((D_HYENA, D_HYENA)), const((D_HYENA + D_ATTN, D)),
            const((1, D)), const((D, D_MEM_ATTN)), const((1, MEM_HEAD_DIM)),
            pl.BlockSpec((1, M, D_MEM_ATTN), lambda i: (i // nt, 0, 0)),
            pl.BlockSpec((1, M, D_MEM_ATTN), lambda i: (i // nt, 0, 0)),
            const((D_MEM_ATTN, D)), const((1, D)),
            const((ROUTE_ROWS, D)), const((ROUTE_ROWS, D)), const((ROUTE_ROWS, 1)),
            const((tm, tm)),
        ],
        out_specs=[
            pl.BlockSpec((tm, D), lambda i: (i, 0)),
            pl.BlockSpec((tm * SLAB, LANE), lambda i: (i, 0)),
            pl.BlockSpec((8, tm), lambda i: (0, i)),
            pl.BlockSpec((1, ROUTE_ROWS, LANE), lambda i: (i, 0, 0)),
        ],
        out_shape=[
            jax.ShapeDtypeStruct((T, D), F32),
            jax.ShapeDtypeStruct((T * SLAB, LANE), U32),
            jax.ShapeDtypeStruct((8, T), F32),
            jax.ShapeDtypeStruct((B * nt, ROUTE_ROWS, LANE), F32),
        ],
        compiler_params=pltpu.CompilerParams(dimension_semantics=("parallel",),
                                             vmem_limit_bytes=VMEM_LIMIT),
        name="mix_mem_route",
    )(xv, y_hy, y_at.reshape(T, D_ATTN), gh, gmat, wo_bf, gx, wq_bf, qg,
      km, vm, wom_bf, gmoe, wr_hi, wr_lo, br, tri)


def _slab_copy(src_ref, src_tok, dst_ref, dst_tok, n, sem):
    s0 = pl.multiple_of(src_tok * SLAB, SLAB)
    d0 = pl.multiple_of(dst_tok * SLAB, SLAB)
    return pltpu.make_async_copy(src_ref.at[pl.ds(s0, n * SLAB)], dst_ref.at[pl.ds(d0, n * SLAB)], sem)


def _index_spec(rows):
    return pl.BlockSpec((1, 1, rows), lambda i: (i, 0, 0), memory_space=pltpu.SMEM)


def _scatter_rows_kernel(idx_ref, src_ref, init_ref, dst_ref, sem, *, rows):
    del init_ref

    def issue(r, carry):
        _slab_copy(src_ref, r, dst_ref, idx_ref[0, 0, r], 1, sem).start()
        return carry

    lax.fori_loop(0, rows, issue, 0)
    _slab_copy(src_ref, 0, dst_ref, 0, rows, sem).wait()


def _scatter_rows(src, dest, n_out, rows):
    T = dest.shape[0]
    init = jnp.zeros((n_out * SLAB, LANE), src.dtype)
    return pl.pallas_call(
        functools.partial(_scatter_rows_kernel, rows=rows),
        grid=(T // rows,),
        in_specs=[_index_spec(rows), pl.BlockSpec((rows * SLAB, LANE), lambda i: (i, 0)),
                  pl.BlockSpec(memory_space=pl.ANY)],
        out_specs=pl.BlockSpec(memory_space=pl.ANY),
        scratch_shapes=[pltpu.SemaphoreType.DMA(())],
        out_shape=jax.ShapeDtypeStruct((n_out * SLAB, LANE), src.dtype),
        input_output_aliases={2: 0},
        compiler_params=pltpu.CompilerParams(dimension_semantics=("arbitrary",),
                                             vmem_limit_bytes=VMEM_LIMIT),
        name="scatter_rows",
    )(dest.reshape(T // rows, 1, rows), src, init)


def _gather_add_kernel(idx_ref, ys_ref, x2_ref, o_ref, buf_ref, sem, *, ct):
    tm = x2_ref.shape[0]
    nr = tm // ct

    def issue(r, carry):
        _slab_copy(ys_ref, idx_ref[0, 0, r], buf_ref, r, 1, sem).start()
        return carry

    lax.fori_loop(0, tm, issue, 0)
    _slab_copy(ys_ref, 0, buf_ref, 0, tm, sem).wait()
    y = jnp.concatenate([buf_ref[pl.ds(j, tm, stride=SLAB), :] for j in range(SLAB)], axis=1)
    res = x2_ref[...] + y
    for c in range(ct):
        o_ref[0, :, c * D_MODEL:(c + 1) * D_MODEL] = res[c * nr:(c + 1) * nr, :]


def _gather_add(ys, dest, x2, B, NR, ct):
    D = D_MODEL
    tm = ct * NR
    nt = (NR * GRID_W) // tm
    T = dest.shape[0]
    return pl.pallas_call(
        functools.partial(_gather_add_kernel, ct=ct),
        grid=(T // tm,),
        in_specs=[_index_spec(tm), pl.BlockSpec(memory_space=pl.ANY), pl.BlockSpec((tm, D), lambda i: (i, 0))],
        out_specs=pl.BlockSpec((1, NR, ct * D), lambda i: (i // nt, 0, i % nt)),
        scratch_shapes=[pltpu.VMEM((tm * SLAB, LANE), F32), pltpu.SemaphoreType.DMA(())],
        out_shape=jax.ShapeDtypeStruct((B, NR, GRID_W * D), F32),
        compiler_params=pltpu.CompilerParams(dimension_semantics=("arbitrary",),
                                             vmem_limit_bytes=VMEM_LIMIT),
        name="gather_add",
    )(dest.reshape(T // tm, 1, tm), ys, x2)


def _moe_kernel(lo_ref, hi_ref, valid_ref, xs_ref, gu_lo_ref, dn_lo_ref, gu_hi_ref, dn_hi_ref, o_ref):
    del lo_ref, hi_ref
    i = pl.program_id(0)
    ts = o_ref.shape[0] // SLAB

    @pl.when(valid_ref[i] > 0)
    def _():
        pieces = []
        for j in range(D_MODEL // (2 * LANE)):
            w = xs_ref[pl.ds(j, ts, stride=SLAB), :]
            pieces += [_unpack_hi(w), _unpack_lo(w)]
        h = jnp.concatenate(pieces, axis=1).astype(BF16)
        gates = lax.bitcast_convert_type(xs_ref[pl.ds(4, ts, stride=SLAB), :], F32)
        y = None
        for col, gu_ref, dn_ref in ((0, gu_lo_ref, dn_lo_ref), (1, gu_hi_ref, dn_hi_ref)):
            gu = jnp.dot(h, gu_ref[0], preferred_element_type=F32)
            act = jax.nn.silu(gu[:, :D_EXPERT]) * gu[:, D_EXPERT:]
            ye = gates[:, col:col + 1] * jnp.dot(act.astype(BF16), dn_ref[0], preferred_element_type=F32)
            y = ye if y is None else y + ye
        for j in range(SLAB):
            o_ref[pl.ds(j, ts, stride=SLAB), :] = y[:, j * LANE:(j + 1) * LANE]

    @pl.when(valid_ref[i] == 0)
    def _():
        o_ref[...] = jnp.zeros_like(o_ref)


def _moe(xs, tile_lo, tile_hi, tile_valid, w_gu_bf, w_dn_bf, ts):
    n_tiles = tile_lo.shape[0]
    D = D_MODEL
    return pl.pallas_call(
        _moe_kernel,
        grid_spec=pltpu.PrefetchScalarGridSpec(
            num_scalar_prefetch=3,
            grid=(n_tiles,),
            in_specs=[
                pl.BlockSpec((ts * SLAB, LANE), lambda i, lo, hi, va: (i, 0)),
                pl.BlockSpec((1, D, 2 * D_EXPERT), lambda i, lo, hi, va: (lo[i], 0, 0)),
                pl.BlockSpec((1, D_EXPERT, D), lambda i, lo, hi, va: (lo[i], 0, 0)),
                pl.BlockSpec((1, D, 2 * D_EXPERT), lambda i, lo, hi, va: (hi[i], 0, 0)),
                pl.BlockSpec((1, D_EXPERT, D), lambda i, lo, hi, va: (hi[i], 0, 0)),
            ],
            out_specs=pl.BlockSpec((ts * SLAB, LANE), lambda i, lo, hi, va: (i, 0)),
        ),
        out_shape=jax.ShapeDtypeStruct((n_tiles * ts * SLAB, LANE), F32),
        compiler_params=pltpu.CompilerParams(dimension_semantics=("arbitrary",),
                                             vmem_limit_bytes=VMEM_LIMIT),
        name="moe_experts",
    )(tile_lo, tile_hi, tile_valid, xs, w_gu_bf, w_dn_bf, w_gu_bf, w_dn_bf)


def _route_tables(route, counts, tm, ts):
    T = route.shape[1]
    cls = route[0].astype(jnp.int32)
    rank = route[1].astype(jnp.int32)
    cnt = counts[:, :N_CLASSES, LANE - 1].astype(jnp.int32)
    total = jnp.sum(cnt, axis=0)
    tiles_c = (total + ts - 1) // ts
    tile_start = jnp.cumsum(tiles_c) - tiles_c
    before = jnp.cumsum(cnt, axis=0) - cnt
    base = tile_start[None, :] * ts + before
    dest = jnp.take_along_axis(base, cls.reshape(-1, tm), axis=1).reshape(T) + rank
    n_tiles = T // ts + N_CLASSES
    tile_id = jnp.arange(n_tiles, dtype=jnp.int32)
    used = jnp.sum(tiles_c)
    tcls = jnp.clip(jnp.searchsorted(jnp.cumsum(tiles_c), tile_id, side="right"), 0, N_CLASSES - 1)
    last_cls = jnp.clip(jnp.searchsorted(jnp.cumsum(tiles_c), used - 1, side="right"), 0, N_CLASSES - 1)
    tcls = jnp.where(tile_id < used, tcls, last_cls).astype(jnp.int32)
    pair_lo = jnp.asarray([p[0] for p in PAIRS], jnp.int32)
    pair_hi = jnp.asarray([p[1] for p in PAIRS], jnp.int32)
    grp, pr = tcls // len(PAIRS), tcls % len(PAIRS)
    tile_lo = grp * EXPERTS_PER_GROUP + pair_lo[pr]
    tile_hi = grp * EXPERTS_PER_GROUP + pair_hi[pr]
    return dest, tile_lo, tile_hi, (tile_id < used).astype(jnp.int32)


def _pick_tile(n, pref):
    t = min(pref, n)
    while n % t:
        t //= 2
    return t


def kernel(x, mem, norm_mix, w_in, hyena_short_w, hyena_short_b, filt_w1, filt_b1, filt_w2, filt_b2, filt_w3, filt_sin_freq, hyena_skip, attn_q_norm, attn_k_norm, out_norm_hyena, out_norm_attn, w_out, norm_xattn, norm_mem, w_q_mem, w_k_mem, w_v_mem, mem_q_norm, mem_k_norm, w_o_mem, norm_moe, w_router_grp, b_router_grp, w_router_exp, b_router_exp, w_gate, w_up, w_down):
    B, L, D = x.shape
    T = B * L
    NR = L // GRID_W
    depth = norm_mix.shape[0]
    ct = _pick_tile(GRID_W, max(1, 512 // NR))
    tm = ct * NR
    tq = _pick_tile(L, 256)
    tk = _pick_tile(L, 512)
    ts = _pick_tile(T, 256)
    rows_per_step = _pick_tile(T, 1024)
    cos_t, sin_t = _rope_tables(L)
    for i in range(depth):
        xv = x.reshape(B, NR, GRID_W * D)
        qkg = jnp.concatenate([jnp.tile(attn_q_norm[i], N_Q_HEADS), jnp.tile(attn_k_norm[i], N_KV_HEADS)])[None, :]
        u, q, k, v = _inproj(xv, norm_mix[i][None, :], w_in[i].astype(BF16), cos_t, sin_t, qkg, ct)
        y_hy = _hyena(u, hyena_short_w[i], hyena_short_b[i], hyena_skip[i], filt_w1[i], filt_b1[i],
                      filt_w2[i], filt_b2[i], filt_w3[i], filt_sin_freq[i], B, L)
        y_at = _attention(q, k, v, out_norm_attn[i][None, :], tq, tk)
        km, vm = _memkv(mem, norm_mem[i][None, :], w_k_mem[i].astype(BF16), w_v_mem[i].astype(BF16),
                        mem_k_norm[i][None, :])
        wr = jnp.concatenate([w_router_grp[i], w_router_exp[i]], axis=1).T
        wr = jnp.pad(wr, ((0, ROUTE_ROWS - wr.shape[0]), (0, 0)))
        wr_hi, wr_lo = _split_bf16(wr)
        br = jnp.pad(jnp.concatenate([b_router_grp[i], b_router_exp[i]]), (0, ROUTE_ROWS - N_GROUPS - N_EXPERTS))
        x2, pay, route, counts = _mix(xv, y_hy, y_at, out_norm_hyena[i][None, :], w_out[i].astype(BF16),
                                      norm_xattn[i][None, :], w_q_mem[i].astype(BF16), mem_q_norm[i][None, :],
                                      km, vm, w_o_mem[i].astype(BF16), norm_moe[i][None, :], wr_hi, wr_lo,
                                      br[:, None], ct)
        dest, tile_lo, tile_hi, tile_valid = _route_tables(route, counts, tm, ts)
        n_tiles = T // ts + N_CLASSES
        xs = _scatter_rows(pay, dest, n_tiles * ts, rows_per_step)
        w_gu = jnp.concatenate([w_gate[i], w_up[i]], axis=-1).astype(BF16)
        ys = _moe(xs, tile_lo, tile_hi, tile_valid, w_gu, w_down[i].astype(BF16), ts)
        x = _gather_add(ys, dest, x2, B, NR, ct).reshape(B, L, D)
    return x
```

```python
import functools
import math

import jax
import jax.numpy as jnp
import numpy as np
from jax import lax
from jax.experimental import pallas as pl
from jax.experimental.pallas import tpu as pltpu

F32 = jnp.float32
BF16 = jnp.bfloat16

D_MODEL = 1024
GRID_W = 64
D_HYENA = 512
HYENA_GROUPS = 8
HYENA_ORDER = 2
SHORT_CONV = 3
FILTER_BANDS = 16
FILTER_FF = 64
DECAY_TARGET = 1e-2
FAST_DECAY_PCT = 0.3
SLOW_DECAY_PCT = 1.5
N_Q_HEADS = 8
N_KV_HEADS = 2
HEAD_DIM = 64
D_ATTN = N_Q_HEADS * HEAD_DIM
D_KV = N_KV_HEADS * HEAD_DIM
ROPE_THETA = 10000.0
D_U = (HYENA_ORDER + 1) * D_HYENA
D_IN = D_U + D_ATTN + 2 * D_KV
D_QK = D_ATTN + D_KV
MEM_HEADS = 4
MEM_HEAD_DIM = 128
D_MEM_ATTN = MEM_HEADS * MEM_HEAD_DIM
N_GROUPS = 4
EXPERTS_PER_GROUP = 4
N_EXPERTS = N_GROUPS * EXPERTS_PER_GROUP
D_EXPERT = 512
EPS = 1e-6

PAIRS = ((0, 1), (0, 2), (0, 3), (1, 2), (1, 3), (2, 3))
N_CLASSES = N_GROUPS * len(PAIRS)
ROUTE_ROWS = 32
LANE = 128
SUBLANE = 8
SLAB = SUBLANE
VMEM_LIMIT = 56 * 1024 * 1024
HY_CB = 256
HY_PAD = 8
HY_UNROLL = 4
DMA_UNROLL = 8
LOG2E = math.log2(math.e)


def _rms(x):
    return x * lax.rsqrt(jnp.mean(x * x, axis=-1, keepdims=True) + EPS)


def _group_sumsq_matrix(width, group):
    idx = np.arange(width) // group
    return jnp.asarray((idx[:, None] == idx[None, :]).astype(np.float32) / group, dtype=BF16)


def _hy_pitch(n1):
    return 4 * n1 + HY_PAD


def _inproj_kernel(x_ref, g_ref, w_ref, cos_ref, sin_ref, qkg_ref, gmat_ref,
                   u_ref, q_ref, k_ref, v_ref):
    x = x_ref[...]
    tm = x.shape[0]
    h = _rms(x) * g_ref[...]
    p = jnp.dot(h.astype(BF16), w_ref[...], preferred_element_type=F32)
    u_ref[...] = p[:, :D_U].astype(BF16)
    qk = p[:, D_U:D_U + D_QK]
    ms = jnp.dot((qk * qk).astype(BF16), gmat_ref[...], preferred_element_type=F32)
    qk = qk * lax.rsqrt(ms + EPS) * qkg_ref[...]
    reps = D_QK // LANE
    cos = jnp.concatenate([cos_ref[...]] * reps, axis=1)
    sin = jnp.concatenate([sin_ref[...]] * reps, axis=1)
    lane = lax.broadcasted_iota(jnp.int32, (tm, D_QK), 1)
    partner = jnp.where((lane % 32) < 16, pltpu.roll(qk, D_QK - 16, 1), pltpu.roll(qk, 16, 1))
    qk = qk * cos + partner * sin
    for hd in range(N_Q_HEADS):
        q_ref[0, hd] = (qk[:, hd * HEAD_DIM:(hd + 1) * HEAD_DIM] * (HEAD_DIM ** -0.5 * LOG2E)).astype(BF16)
    for hd in range(N_KV_HEADS):
        k_ref[0, hd] = qk[:, D_ATTN + hd * HEAD_DIM:D_ATTN + (hd + 1) * HEAD_DIM].astype(BF16)
    vv = p[:, D_U + D_QK:]
    lane_v = lax.broadcasted_iota(jnp.int32, (tm, LANE), 1)
    ones_col = jnp.where(lane_v == HEAD_DIM, 1.0, 0.0)
    v_ref[0, 0] = jnp.where(lane_v < HEAD_DIM, vv, ones_col).astype(BF16)
    v_ref[0, 1] = jnp.where(lane_v < HEAD_DIM, pltpu.roll(vv, HEAD_DIM, 1), ones_col).astype(BF16)


def _inproj(x, g, w_bf, cos_t, sin_t, qkg, tm):
    B, L, D = x.shape
    nt = L // tm
    gmat = _group_sumsq_matrix(D_QK, HEAD_DIM)
    return pl.pallas_call(
        _inproj_kernel,
        grid=(B * nt,),
        in_specs=[
            pl.BlockSpec((tm, D), lambda i: (i, 0)),
            pl.BlockSpec((1, D), lambda i: (0, 0)),
            pl.BlockSpec((D, D_IN), lambda i: (0, 0)),
            pl.BlockSpec((tm, LANE), lambda i: (i % nt, 0)),
            pl.BlockSpec((tm, LANE), lambda i: (i % nt, 0)),
            pl.BlockSpec((1, D_QK), lambda i: (0, 0)),
            pl.BlockSpec((D_QK, D_QK), lambda i: (0, 0)),
        ],
        out_specs=[
            pl.BlockSpec((tm, D_U), lambda i: (i, 0)),
            pl.BlockSpec((1, N_Q_HEADS, tm, HEAD_DIM), lambda i: (i // nt, 0, i % nt, 0)),
            pl.BlockSpec((1, N_KV_HEADS, tm, HEAD_DIM), lambda i: (i // nt, 0, i % nt, 0)),
            pl.BlockSpec((1, N_KV_HEADS, tm, LANE), lambda i: (i // nt, 0, i % nt, 0)),
        ],
        out_shape=[
            jax.ShapeDtypeStruct((B * L, D_U), BF16),
            jax.ShapeDtypeStruct((B, N_Q_HEADS, L, HEAD_DIM), BF16),
            jax.ShapeDtypeStruct((B, N_KV_HEADS, L, HEAD_DIM), BF16),
            jax.ShapeDtypeStruct((B, N_KV_HEADS, L, LANE), BF16),
        ],
        compiler_params=pltpu.CompilerParams(dimension_semantics=("parallel",),
                                             vmem_limit_bytes=VMEM_LIMIT),
        name="inproj",
    )(x.reshape(B * L, D), g, w_bf, cos_t, sin_t, qkg, gmat)


def _rope_tables(L):
    p = np.arange(L)
    row, col = p // GRID_W, p % GRID_W
    d = np.arange(LANE) % HEAD_DIM
    sec, r = d // 32, d % 32
    inv = (ROPE_THETA ** (-(r % 16).astype(np.float64) / 16.0)).astype(np.float32)
    pos = np.where(sec[None, :] == 0, row[:, None], col[:, None]).astype(np.float32)
    ang = pos * inv[None, :]
    sign = np.where(r < 16, -1.0, 1.0)[None, :]
    return jnp.asarray(np.cos(ang), F32), jnp.asarray(np.sin(ang) * sign, F32)


def _real_embed(m):
    return np.block([[m.real, -m.imag], [m.imag, m.real]])


@functools.lru_cache(maxsize=None)
def _dft_tables(n1, n2):
    L = n1 * n2
    i1, i2 = np.arange(n1), np.arange(n2)
    om = np.exp(-1j * np.pi / L)
    f1 = np.exp(-2j * np.pi * np.outer(i1, i1) / n1)
    tw = np.exp(-2j * np.pi * np.outer(i1, i2) / L)
    fwd1, fwd1_real = [], []
    for t in i2:
        ev = tw[:, t][:, None] * f1
        od = (om ** t) * tw[:, t][:, None] * f1 * (om ** (n2 * i1))[None, :]
        fwd1.append(np.concatenate([_real_embed(ev), _real_embed(od)], axis=0))
        z = np.zeros((n1, n1))
        fwd1_real.append(np.block([[ev.real, z], [ev.imag, z], [z, od.real], [z, od.imag]]))
    f2 = _real_embed(np.exp(-2j * np.pi * np.outer(i2, i2) / n2))
    f2i = np.exp(2j * np.pi * np.outer(i2, i2) / n2)
    inv_e = np.stack([_real_embed(np.conj(tw[k, :])[:, None] * f2i) for k in i1])
    inv_o = np.stack([_real_embed((om ** (-i2))[:, None] * np.conj(tw[k, :])[:, None] * f2i) for k in i1])
    f1i = np.exp(2j * np.pi * np.outer(i1, i1) / n1)
    inv2 = np.concatenate([_real_embed(f1i), _real_embed((om ** (-n2 * i1))[:, None] * f1i)], axis=1) / (2 * L)
    as_f32 = lambda a: np.asarray(a, np.float32)
    return {"fwd1": as_f32(np.stack(fwd1)), "fwd1_real": as_f32(np.stack(fwd1_real)), "fwd2": as_f32(f2),
            "inv_e": as_f32(inv_e), "inv_o": as_f32(inv_o), "inv2": as_f32(inv2)}


def _conv_block(ref, w_ref, b_ref, base, t2, n1, n2):
    def blk(i):
        return ref[pl.ds(pl.multiple_of(base + i * n1, n1), n1), :].astype(F32)

    cur, prev, nxt = blk(t2), blk((t2 + n2 - 1) % n2), blk((t2 + 1) % n2)
    row = lax.broadcasted_iota(jnp.int32, cur.shape, 0)
    prev_wrapped = jnp.where(row == 0, 0.0, pltpu.roll(prev, 1, 0))
    nxt_wrapped = jnp.where(row == n1 - 1, 0.0, pltpu.roll(nxt, n1 - 1, 0))
    prev = jnp.where(t2 == 0, prev_wrapped, prev)
    nxt = jnp.where(t2 == n2 - 1, nxt_wrapped, nxt)
    return prev * w_ref[0:1, :] + cur * w_ref[1:2, :] + nxt * w_ref[2:3, :] + b_ref[...]


def _plain_block(ref, base, t2, n1):
    return ref[pl.ds(pl.multiple_of(base + t2 * n1, n1), n1), :].astype(F32)


def _store_slabs(a_ref, rows, val):
    for s in range(val.shape[1] // LANE):
        a_ref[s, rows, :] = val[:, s * LANE:(s + 1) * LANE]


def _load_slabs(a_ref, rows):
    return jnp.concatenate([a_ref[s, rows, :] for s in range(a_ref.shape[0])], axis=1)


def _forward_stage1(load_stacked, fwd1_ref, a_ref, n1, n2):
    pitch = _hy_pitch(n1)

    def body(t2, carry):
        a = jnp.dot(fwd1_ref[t2], load_stacked(t2).astype(BF16), preferred_element_type=F32)
        _store_slabs(a_ref, pl.ds(pl.multiple_of(t2 * pitch, SUBLANE), 4 * n1), a)
        return carry

    lax.fori_loop(0, n2, body, 0, unroll=HY_UNROLL)


def _forward_stage2(a_ref, fwd2_ref, k1s, n1, n2):
    pitch = _hy_pitch(n1)
    cols = []
    for k1 in k1s:
        for half in range(2):
            cols.append(jnp.concatenate(
                [_load_slabs(a_ref, pl.ds((2 * half + part) * n1 + k1, n2, stride=pitch)) for part in range(2)],
                axis=0))
    x = jnp.dot(fwd2_ref[...], jnp.concatenate(cols, axis=1).astype(BF16), preferred_element_type=F32)
    w2 = x.shape[1] // len(k1s)
    return [(x[:n2, i * w2:(i + 1) * w2], x[n2:, i * w2:(i + 1) * w2]) for i in range(len(k1s))]


def _hyena_kernel(*refs, n1, n2, conv_z):
    L = n1 * n2
    if conv_z:
        (z_ref, zw_ref, zb_ref, g_ref, gw_ref, gb_ref, ker_ref, kei_ref, kor_ref, koi_ref,
         skip_ref, fwd1_ref, fwd2_ref, inve_ref, invo_ref, inv2_ref, o_ref, a_ref) = refs
        load_z = lambda part, t2: _conv_block(z_ref, zw_ref, zb_ref, part * L, t2, n1, n2)
    else:
        (z_ref, g_ref, gw_ref, gb_ref, ker_ref, kei_ref, kor_ref, koi_ref,
         skip_ref, fwd1_ref, fwd2_ref, inve_ref, invo_ref, inv2_ref, o_ref, a_ref) = refs
        load_z = lambda part, t2: _plain_block(z_ref, part * L, t2, n1)
    cb = z_ref.shape[1]
    pitch = _hy_pitch(n1)

    _forward_stage1(lambda t2: jnp.concatenate([load_z(0, t2), load_z(1, t2)], axis=0),
                    fwd1_ref, a_ref, n1, n2)

    skip2 = jnp.concatenate([skip_ref[...], skip_ref[...]], axis=1)
    group = HY_UNROLL if n1 % HY_UNROLL == 0 else 1

    def freq_body(i, carry):
        k1s = [i * group + j for j in range(group)]
        results = []
        for k1, (xr, xi) in zip(k1s, _forward_stage2(a_ref, fwd2_ref, k1s, n1, n2)):
            rows = pl.ds(pl.multiple_of(k1 * n2, n2), n2)
            kr = jnp.concatenate([ker_ref[rows, :], kor_ref[rows, :]], axis=1).astype(F32) + skip2
            ki = jnp.concatenate([kei_ref[rows, :], koi_ref[rows, :]], axis=1).astype(F32)
            yr = xr * kr - xi * ki
            yi = xr * ki + xi * kr
            ye = jnp.concatenate([yr[:, :cb], yi[:, :cb]], axis=0).astype(BF16)
            yo = jnp.concatenate([yr[:, cb:], yi[:, cb:]], axis=0).astype(BF16)
            be = jnp.dot(inve_ref[k1], ye, preferred_element_type=F32)
            bo = jnp.dot(invo_ref[k1], yo, preferred_element_type=F32)
            results.append((be[:n2], be[n2:], bo[:n2], bo[n2:]))
        for k1, quarters in zip(k1s, results):
            for q, val in enumerate(quarters):
                _store_slabs(a_ref, pl.ds(q * n1 + k1, n2, stride=pitch), val)
        return carry

    lax.fori_loop(0, n1 // group, freq_body, 0)

    def time_body(t2, carry):
        blk = _load_slabs(a_ref, pl.ds(pl.multiple_of(t2 * pitch, SUBLANE), 4 * n1))
        y = jnp.dot(inv2_ref[...], blk.astype(BF16), preferred_element_type=F32)
        for part in range(2):
            gate = _conv_block(g_ref, gw_ref, gb_ref, part * L, t2, n1, n2)
            o_ref[pl.ds(pl.multiple_of(part * L + t2 * n1, n1), n1), :] = (
                gate * y[part * n1:(part + 1) * n1]).astype(o_ref.dtype)
        return carry

    lax.fori_loop(0, n2, time_body, 0, unroll=HY_UNROLL)


def _hyena_order(z, z_col0, u, gate_col0, short_w, short_b, spectra, spec_col0, skip, tabs, B, L, conv_z):
    n2 = GRID_W
    n1 = L // n2
    cb = HY_CB
    ncb = D_HYENA // cb
    pitch = _hy_pitch(n1)
    col = lambda off: (lambda c, p: (0, off + c))
    pair = lambda off: (lambda c, p: (p, off + c))
    one = pl.Buffered(1)
    const3 = lambda a: pl.BlockSpec(a.shape, lambda c, p: (0, 0, 0))
    const2 = lambda a: pl.BlockSpec(a.shape, lambda c, p: (0, 0))
    in_specs = [pl.BlockSpec((2 * L, cb), pair(z_col0), pipeline_mode=one)]
    args = [z]
    if conv_z:
        in_specs += [pl.BlockSpec((SHORT_CONV, cb), col(z_col0)), pl.BlockSpec((1, cb), col(z_col0))]
        args += [short_w, short_b]
    in_specs += [pl.BlockSpec((2 * L, cb), pair(gate_col0), pipeline_mode=one),
                 pl.BlockSpec((SHORT_CONV, cb), col(gate_col0)), pl.BlockSpec((1, cb), col(gate_col0))]
    args += [u, short_w, short_b]
    in_specs += [pl.BlockSpec((L, cb), col(spec_col0), pipeline_mode=one)] * 4
    args += list(spectra)
    in_specs += [pl.BlockSpec((1, cb), lambda c, p: (0, c)),
                 const3(tabs["fwd1"]), const2(tabs["fwd2"]), const3(tabs["inv_e"]), const3(tabs["inv_o"]),
                 const2(tabs["inv2"])]
    args += [skip, tabs["fwd1"], tabs["fwd2"], tabs["inv_e"], tabs["inv_o"], tabs["inv2"]]
    return pl.pallas_call(
        functools.partial(_hyena_kernel, n1=n1, n2=n2, conv_z=conv_z),
        grid=(ncb, B // 2),
        in_specs=in_specs,
        out_specs=pl.BlockSpec((2 * L, cb), lambda c, p: (p, c)),
        out_shape=jax.ShapeDtypeStruct((B * L, D_HYENA), BF16),
        scratch_shapes=[pltpu.VMEM((cb // LANE, n2 * pitch, LANE), F32)],
        compiler_params=pltpu.CompilerParams(dimension_semantics=("arbitrary", "arbitrary"),
                                             vmem_limit_bytes=VMEM_LIMIT),
        name="hyena_order",
    )(*args)


def _spectra_kernel(kc_ref, kn_ref, fwd1_ref, fwd2_ref, er_ref, ei_ref, or_ref, oi_ref, a_ref, *, n1, n2):
    cb = kc_ref.shape[1]
    _forward_stage1(lambda t2: jnp.concatenate([_plain_block(kc_ref, 0, t2, n1), _plain_block(kn_ref, 0, t2, n1)],
                                               axis=0), fwd1_ref, a_ref, n1, n2)

    def freq_body(k1, carry):
        (xr, xi), = _forward_stage2(a_ref, fwd2_ref, [k1], n1, n2)
        rows = pl.ds(pl.multiple_of(k1 * n2, n2), n2)
        er_ref[rows, :] = xr[:, :cb].astype(BF16)
        ei_ref[rows, :] = xi[:, :cb].astype(BF16)
        or_ref[rows, :] = xr[:, cb:].astype(BF16)
        oi_ref[rows, :] = xi[:, cb:].astype(BF16)
        return carry

    lax.fori_loop(0, n1, freq_body, 0, unroll=HY_UNROLL)


def _filter_spectra(kc, kn, tabs, L):
    n2 = GRID_W
    n1 = L // n2
    cb = HY_CB
    C = kc.shape[1]
    pitch = _hy_pitch(n1)
    return pl.pallas_call(
        functools.partial(_spectra_kernel, n1=n1, n2=n2),
        grid=(C // cb,),
        in_specs=[pl.BlockSpec((L, cb), lambda c: (0, c)), pl.BlockSpec((L, cb), lambda c: (0, c)),
                  pl.BlockSpec(tabs["fwd1_real"].shape, lambda c: (0, 0, 0)),
                  pl.BlockSpec(tabs["fwd2"].shape, lambda c: (0, 0))],
        out_specs=[pl.BlockSpec((L, cb), lambda c: (0, c))] * 4,
        out_shape=[jax.ShapeDtypeStruct((L, C), BF16)] * 4,
        scratch_shapes=[pltpu.VMEM((cb // LANE, n2 * pitch, LANE), F32)],
        compiler_params=pltpu.CompilerParams(dimension_semantics=("arbitrary",),
                                             vmem_limit_bytes=VMEM_LIMIT),
        name="filter_spectra",
    )(kc, kn, tabs["fwd1_real"], tabs["fwd2"])


def _filter_time_kernel(zf_ref, zr_ref, w1_ref, b1_ref, w2_ref, b2_ref, w3f_ref, w3b_ref, sf_ref,
                        dl_ref, kc_ref, kn_ref, hf_ref, hr_ref):
    hp = lax.Precision.HIGHEST
    df = jnp.exp(-zf_ref[:, 0:1] * dl_ref[...])
    row = lax.broadcasted_iota(jnp.int32, df.shape, 0)
    dr = jnp.where(row > 0, jnp.exp(-zr_ref[:, 0:1] * dl_ref[...]), 0.0)

    @pl.when(pl.program_id(0) == 0)
    def _():
        for z_ref, h_ref in ((zf_ref, hf_ref), (zr_ref, hr_ref)):
            a = jnp.sin(sf_ref[0:1, :] * (jnp.dot(z_ref[...], w1_ref[...], precision=hp,
                                                  preferred_element_type=F32) + b1_ref[...]))
            h_ref[...] = jnp.sin(sf_ref[1:2, :] * (jnp.dot(a, w2_ref[...], precision=hp,
                                                           preferred_element_type=F32) + b2_ref[...]))

    fwd = jnp.dot(hf_ref[...], w3f_ref[...], precision=hp, preferred_element_type=F32) * df
    bwd = jnp.dot(hr_ref[...], w3b_ref[...], precision=hp, preferred_element_type=F32) * dr
    norm = (jnp.sum(jnp.abs(fwd), axis=0, keepdims=True) + jnp.sum(jnp.abs(bwd), axis=0, keepdims=True)
            + EPS)
    kc_ref[...] = (fwd + bwd) / norm
    kn_ref[...] = (fwd - bwd) / norm


def _filter_time(w1, b1, w2, b2, w3, sin_freq, L):
    NR = L // GRID_W
    f64 = np.float64
    p = np.arange(L)
    t_nat = (p % NR) * GRID_W + p // NR
    tl = np.linspace(0.0, 1.0, L)
    bands = np.linspace(1e-4, FILTER_BANDS - 1, FILTER_BANDS)
    wv = (2.0 * math.pi / L) * np.arange(L, dtype=f64)
    feats = np.concatenate([tl[:, None], np.cos(bands[None, :] * wv[:, None]),
                            np.sin(bands[None, :] * wv[:, None])], axis=1)
    deltas = np.abs(np.linspace(math.log(DECAY_TARGET) / SLOW_DECAY_PCT,
                                math.log(DECAY_TARGET) / FAST_DECAY_PCT, D_HYENA))
    t_rev = (L - t_nat) % L
    pad = LANE - feats.shape[1]
    zf = np.pad(feats[t_nat], ((0, 0), (0, pad)))
    zr = np.pad(feats[t_rev], ((0, 0), (0, pad)))
    dl = np.tile(deltas, HYENA_ORDER)[None, :]
    w1p = jnp.pad(w1, ((0, pad), (0, 0)))
    w3r = w3.reshape(FILTER_FF, HYENA_ORDER, 2, D_HYENA)
    w3f = w3r[:, :, 0].reshape(FILTER_FF, HYENA_ORDER * D_HYENA)
    w3b = w3r[:, :, 1].reshape(FILTER_FF, HYENA_ORDER * D_HYENA)
    C = HYENA_ORDER * D_HYENA
    cb = HY_CB
    full = lambda a: pl.BlockSpec(a.shape, lambda c: (0,) * a.ndim)
    colb = lambda r: pl.BlockSpec((r, cb), lambda c: (0, c))
    zf, zr, dl = (jnp.asarray(a, F32) for a in (zf, zr, dl))
    args = (zf, zr, w1p, b1[None, :], w2, b2[None, :], w3f, w3b, sin_freq, dl)
    return pl.pallas_call(
        _filter_time_kernel,
        grid=(C // cb,),
        in_specs=[full(zf), full(zr), full(w1p), full(args[3]), full(w2), full(args[5]),
                  colb(FILTER_FF), colb(FILTER_FF), full(sin_freq), colb(1)],
        out_specs=[colb(L), colb(L)],
        out_shape=[jax.ShapeDtypeStruct((L, C), F32)] * 2,
        scratch_shapes=[pltpu.VMEM((L, FILTER_FF), F32)] * 2,
        compiler_params=pltpu.CompilerParams(dimension_semantics=("arbitrary",),
                                             vmem_limit_bytes=VMEM_LIMIT),
        name="filter_time",
    )(*args)


def _hyena(u, short_w, short_b, skip, w1, b1, w2, b2, w3, sin_freq, B, L):
    tabs = {name: jnp.asarray(t).astype(BF16) for name, t in _dft_tables(L // GRID_W, GRID_W).items()}
    kc, kn = _filter_time(w1, b1, w2, b2, w3, sin_freq, L)
    spectra = _filter_spectra(kc, kn, tabs, L)
    ncb = D_HYENA // HY_CB
    sb = short_b[None, :]
    z1 = _hyena_order(u, 0, u, ncb, short_w, sb, spectra, 0, skip[0:1], tabs, B, L, True)
    return _hyena_order(z1, 0, u, 2 * ncb, short_w, sb, spectra, ncb, skip[1:2], tabs, B, L, False)


def _attn_kernel(q_ref, k_ref, v_ref, g_ref, st_ref, o_ref, *, tk):
    G, tq = q_ref.shape[1], q_ref.shape[2]
    L = k_ref.shape[2]
    R = G * tq
    q = q_ref[0].reshape(R, HEAD_DIM)
    m = jnp.full((R, 1), -jnp.inf, F32)
    acc = jnp.zeros((R, LANE), F32)
    for c in range(L // tk):
        kc = k_ref[0, 0, c * tk:(c + 1) * tk, :]
        s = lax.dot_general(q, kc, (((1,), (1,)), ((), ())), preferred_element_type=F32)
        m_new = jnp.maximum(m, jnp.max(s, axis=-1, keepdims=True))
        p = jnp.exp2((s - m_new).astype(BF16))
        acc = jnp.exp2(m - m_new) * acc + jnp.dot(p, v_ref[0, 0, c * tk:(c + 1) * tk, :],
                                                   preferred_element_type=F32)
        m = m_new
    a_hi, a_lo = _split_bf16(acc * acc)
    ms = (jnp.dot(a_hi, st_ref[:, :LANE], preferred_element_type=F32)
          + jnp.dot(a_lo, st_ref[:, :LANE], preferred_element_type=F32))
    l_hi, l_lo = _split_bf16(acc)
    l = (jnp.dot(l_hi, st_ref[:, LANE:], preferred_element_type=F32)
         + jnp.dot(l_lo, st_ref[:, LANE:], preferred_element_type=F32))
    o = (acc * lax.rsqrt(ms + EPS * l * l))[:, :HEAD_DIM]
    o = jnp.concatenate([o[h * tq:(h + 1) * tq] for h in range(G)], axis=1)
    o_ref[0] = (o * g_ref[...]).astype(o_ref.dtype)


def _attention(q, k, v, g_out, tq, tk):
    B, H, L, _ = q.shape
    G = H // N_KV_HEADS
    st = np.zeros((LANE, 2 * LANE), np.float32)
    st[:HEAD_DIM, :LANE] = 1.0 / HEAD_DIM
    st[HEAD_DIM, LANE:] = 1.0
    st = jnp.asarray(st, dtype=BF16)
    return pl.pallas_call(
        functools.partial(_attn_kernel, tk=tk),
        grid=(B, N_KV_HEADS, L // tq),
        in_specs=[
            pl.BlockSpec((1, G, tq, HEAD_DIM), lambda b, kv, i: (b, kv, i, 0)),
            pl.BlockSpec((1, 1, L, HEAD_DIM), lambda b, kv, i: (b, kv, 0, 0)),
            pl.BlockSpec((1, 1, L, LANE), lambda b, kv, i: (b, kv, 0, 0)),
            pl.BlockSpec((1, G * HEAD_DIM), lambda b, kv, i: (0, kv)),
            pl.BlockSpec((LANE, 2 * LANE), lambda b, kv, i: (0, 0)),
        ],
        out_specs=pl.BlockSpec((1, tq, G * HEAD_DIM), lambda b, kv, i: (b, i, kv)),
        out_shape=jax.ShapeDtypeStruct((B, L, D_ATTN), BF16),
        compiler_params=pltpu.CompilerParams(dimension_semantics=("parallel", "parallel", "parallel"),
                                             vmem_limit_bytes=VMEM_LIMIT),
        name="attention",
    )(q, k, v, g_out, st)


def _memkv_kernel(m_ref, g_ref, wk_ref, wv_ref, kg_ref, k_ref, v_ref):
    h = (_rms(m_ref[0]) * g_ref[...]).astype(BF16)
    k = jnp.dot(h, wk_ref[...], preferred_element_type=F32)
    v = jnp.dot(h, wv_ref[...], preferred_element_type=F32)
    kn = [_rms(k[:, i * MEM_HEAD_DIM:(i + 1) * MEM_HEAD_DIM]) * kg_ref[...] for i in range(MEM_HEADS)]
    k_ref[0] = jnp.concatenate(kn, axis=1).astype(BF16)
    v_ref[0] = v.astype(BF16)


def _memkv(mem, g, wk_bf, wv_bf, kg):
    B, M, D = mem.shape
    return pl.pallas_call(
        _memkv_kernel,
        grid=(B,),
        in_specs=[
            pl.BlockSpec((1, M, D), lambda b: (b, 0, 0)),
            pl.BlockSpec((1, D), lambda b: (0, 0)),
            pl.BlockSpec((D, D_MEM_ATTN), lambda b: (0, 0)),
            pl.BlockSpec((D, D_MEM_ATTN), lambda b: (0, 0)),
            pl.BlockSpec((1, MEM_HEAD_DIM), lambda b: (0, 0)),
        ],
        out_specs=[pl.BlockSpec((1, M, D_MEM_ATTN), lambda b: (b, 0, 0))] * 2,
        out_shape=[jax.ShapeDtypeStruct((B, M, D_MEM_ATTN), BF16)] * 2,
        compiler_params=pltpu.CompilerParams(dimension_semantics=("parallel",),
                                             vmem_limit_bytes=VMEM_LIMIT),
        name="memkv",
    )(mem, g, wk_bf, wv_bf, kg)


def _split_bf16(a):
    hi = a.astype(BF16)
    return hi, (a - hi.astype(F32)).astype(BF16)


def _router_logits(h, wrh_ref, wrl_ref, br_ref):
    h_hi, h_lo = _split_bf16(h)
    dn = (((1,), (1,)), ((), ()))
    return (lax.dot_general(wrh_ref[...], h_hi, dn, preferred_element_type=F32)
            + lax.dot_general(wrh_ref[...], h_lo, dn, preferred_element_type=F32)
            + lax.dot_general(wrl_ref[...], h_hi, dn, preferred_element_type=F32)) + br_ref[...]


def _mix_kernel(x_ref, yh_ref, ya_ref, gh_ref, gmat_ref, wo_ref, gx_ref, wq_ref, qg_ref,
                km_ref, vm_ref, wom_ref, gmoe_ref, wrh_ref, wrl_ref, br_ref, tri_ref,
                x2_ref, route_ref, cnt_ref):
    x = x_ref[...]
    tm = x.shape[0]
    yh = yh_ref[...].astype(F32)
    ms = jnp.dot((yh * yh).astype(BF16), gmat_ref[...], preferred_element_type=F32)
    yh = yh * lax.rsqrt(ms + EPS) * gh_ref[...]
    x1 = (x
          + jnp.dot(yh.astype(BF16), wo_ref[:D_HYENA, :], preferred_element_type=F32)
          + jnp.dot(ya_ref[...], wo_ref[D_HYENA:, :], preferred_element_type=F32))
    h2 = (_rms(x1) * gx_ref[...]).astype(BF16)
    qm = jnp.dot(h2, wq_ref[...], preferred_element_type=F32)
    heads = []
    for i in range(MEM_HEADS):
        sl = slice(i * MEM_HEAD_DIM, (i + 1) * MEM_HEAD_DIM)
        qn = (_rms(qm[:, sl]) * qg_ref[...] * (MEM_HEAD_DIM ** -0.5)).astype(BF16)
        s = lax.dot_general(qn, km_ref[0, :, sl], (((1,), (1,)), ((), ())), preferred_element_type=F32)
        p = jnp.exp(s - jnp.max(s, axis=-1, keepdims=True))
        o = jnp.dot(p.astype(BF16), vm_ref[0, :, sl], preferred_element_type=F32)
        heads.append(o / jnp.sum(p, axis=-1, keepdims=True))
    om = jnp.concatenate(heads, axis=1).astype(BF16)
    x2 = x1 + jnp.dot(om, wom_ref[...], preferred_element_type=F32)
    for j in range(SLAB):
        x2_ref[pl.ds(j, tm, stride=SLAB), :] = x2[:, j * LANE:(j + 1) * LANE]

    lt = _router_logits(_rms(x2) * gmoe_ref[...], wrh_ref, wrl_ref, br_ref)
    g = [lt[i:i + 1, :] for i in range(N_GROUPS)]
    gmax = jnp.maximum(jnp.maximum(g[0], g[1]), jnp.maximum(g[2], g[3]))
    gidx = jnp.where(g[0] == gmax, 0, jnp.where(g[1] == gmax, 1, jnp.where(g[2] == gmax, 2, 3)))
    sel = []
    for j in range(EXPERTS_PER_GROUP):
        rows = [lt[N_GROUPS + EXPERTS_PER_GROUP * i + j:N_GROUPS + EXPERTS_PER_GROUP * i + j + 1, :]
                for i in range(N_GROUPS)]
        sel.append(jnp.where(gidx == 0, rows[0], jnp.where(gidx == 1, rows[1],
                                                            jnp.where(gidx == 2, rows[2], rows[3]))))
    def first_argmax(vals):
        best = jnp.maximum(jnp.maximum(vals[0], vals[1]), jnp.maximum(vals[2], vals[3]))
        return jnp.where(vals[0] == best, 0, jnp.where(vals[1] == best, 1, jnp.where(vals[2] == best, 2, 3)))

    a = first_argmax(sel)
    b = first_argmax([jnp.where(a == j, -jnp.inf, sel[j]) for j in range(EXPERTS_PER_GROUP)])
    lo, hi = jnp.minimum(a, b), jnp.maximum(a, b)
    pair = jnp.where(lo == 0, hi - 1, jnp.where(lo == 1, hi + 1, 5))
    cls = gidx * len(PAIRS) + pair

    rows_i = lax.broadcasted_iota(jnp.int32, (ROUTE_ROWS, tm), 0)
    onehot = rows_i == cls
    prefix = jnp.dot(jnp.where(onehot, 1.0, 0.0).astype(BF16), tri_ref[...], preferred_element_type=F32)
    rank = jnp.sum(jnp.where(onehot, prefix, 0.0), axis=0, keepdims=True) - 1.0
    rr = lax.broadcasted_iota(jnp.int32, (8, tm), 0)
    route_ref[...] = jnp.where(rr == 0, cls.astype(F32), jnp.where(rr == 1, rank, 0.0))
    cnt_ref[0] = prefix[:, tm - LANE:]


def _mix(x, y_hy, y_at, gh, wo_bf, gx, wq_bf, qg, km, vm, wom_bf, gmoe, wr_hi, wr_lo, br, tm):
    B, L, D = x.shape
    nt = L // tm
    T = B * L
    M = km.shape[1]
    gmat = _group_sumsq_matrix(D_HYENA, D_HYENA // HYENA_GROUPS)
    tri = jnp.asarray(np.triu(np.ones((tm, tm), np.float32)), dtype=BF16)
    const = lambda shape: pl.BlockSpec(shape, lambda i: (0,) * len(shape))
    return pl.pallas_call(
        _mix_kernel,
        grid=(B * nt,),
        in_specs=[
            pl.BlockSpec((tm, D), lambda i: (i, 0)),
            pl.BlockSpec((tm, D_HYENA), lambda i: (i, 0)),
            pl.BlockSpec((tm, D_ATTN), lambda i: (i, 0)),
            const((1, D_HYENA)), const((D_HYENA, D_HYENA)), const((D_HYENA + D_ATTN, D)),
            const((1, D)), const((D, D_MEM_ATTN)), const((1, MEM_HEAD_DIM)),
            pl.BlockSpec((1, M, D_MEM_ATTN), lambda i: (i // nt, 0, 0)),
            pl.BlockSpec((1, M, D_MEM_ATTN), lambda i: (i // nt, 0, 0)),
            const((D_MEM_ATTN, D)), const((1, D)),
            const((ROUTE_ROWS, D)), const((ROUTE_ROWS, D)), const((ROUTE_ROWS, 1)),
            const((tm, tm)),
        ],
        out_specs=[
            pl.BlockSpec((tm * SLAB, LANE), lambda i: (i, 0)),
            pl.BlockSpec((8, tm), lambda i: (0, i)),
            pl.BlockSpec((1, ROUTE_ROWS, LANE), lambda i: (i, 0, 0)),
        ],
        out_shape=[
            jax.ShapeDtypeStruct((T * SLAB, LANE), F32),
            jax.ShapeDtypeStruct((8, T), F32),
            jax.ShapeDtypeStruct((B * nt, ROUTE_ROWS, LANE), F32),
        ],
        compiler_params=pltpu.CompilerParams(dimension_semantics=("parallel",),
                                             vmem_limit_bytes=VMEM_LIMIT),
        name="mix_mem_route",
    )(x.reshape(T, D), y_hy, y_at.reshape(T, D_ATTN), gh, gmat, wo_bf, gx, wq_bf, qg,
      km, vm, wom_bf, gmoe, wr_hi, wr_lo, br, tri)


def _slab_copy(src_ref, src_tok, dst_ref, dst_tok, n, sem):
    s0 = pl.multiple_of(src_tok * SLAB, SLAB)
    d0 = pl.multiple_of(dst_tok * SLAB, SLAB)
    return pltpu.make_async_copy(src_ref.at[pl.ds(s0, n * SLAB)], dst_ref.at[pl.ds(d0, n * SLAB)], sem)


def _index_spec(rows):
    return pl.BlockSpec((1, 1, rows), lambda i: (i, 0, 0), memory_space=pltpu.SMEM)


def _scatter_rows_kernel(idx_ref, src_ref, init_ref, dst_ref, sem, *, rows):
    del init_ref

    def issue(r, carry):
        _slab_copy(src_ref, r, dst_ref, idx_ref[0, 0, r], 1, sem).start()
        return carry

    lax.fori_loop(0, rows, issue, 0, unroll=DMA_UNROLL)
    _slab_copy(src_ref, 0, dst_ref, 0, rows, sem).wait()


def _scatter_rows(src, dest, n_out, rows):
    T = dest.shape[0]
    init = jnp.zeros((n_out * SLAB, LANE), src.dtype)
    return pl.pallas_call(
        functools.partial(_scatter_rows_kernel, rows=rows),
        grid=(T // rows,),
        in_specs=[_index_spec(rows), pl.BlockSpec((rows * SLAB, LANE), lambda i: (i, 0)),
                  pl.BlockSpec(memory_space=pl.ANY)],
        out_specs=pl.BlockSpec(memory_space=pl.ANY),
        scratch_shapes=[pltpu.SemaphoreType.DMA(())],
        out_shape=jax.ShapeDtypeStruct((n_out * SLAB, LANE), src.dtype),
        input_output_aliases={2: 0},
        compiler_params=pltpu.CompilerParams(dimension_semantics=("arbitrary",),
                                             vmem_limit_bytes=VMEM_LIMIT),
        name="scatter_rows",
    )(dest.reshape(T // rows, 1, rows), src, init)


def _gather_rows_kernel(idx_ref, idx_next_ref, ys_ref, o_ref, buf_ref, sem):
    i = pl.program_id(0)
    tm = o_ref.shape[0]

    def issue(tile_idx_ref, slot):
        def body(r, carry):
            _slab_copy(ys_ref, tile_idx_ref[0, 0, r], buf_ref.at[slot], r, 1, sem.at[slot]).start()
            return carry

        lax.fori_loop(0, tm, body, 0, unroll=DMA_UNROLL)

    def step(slot):
        @pl.when(i == 0)
        def _():
            issue(idx_ref, slot)

        @pl.when(i + 1 < pl.num_programs(0))
        def _():
            issue(idx_next_ref, 1 - slot)

        _slab_copy(ys_ref, 0, buf_ref.at[slot], 0, tm, sem.at[slot]).wait()
        o_ref[...] = jnp.concatenate([buf_ref[slot, pl.ds(j, tm, stride=SLAB), :] for j in range(SLAB)],
                                     axis=1)

    for slot in range(2):
        pl.when(i % 2 == slot)(functools.partial(step, slot))


def _gather_rows(ys, dest, tm):
    T = dest.shape[0]
    D = SLAB * LANE
    n = T // tm
    idx = dest.reshape(n, 1, tm)
    next_spec = pl.BlockSpec((1, 1, tm), lambda i: (jnp.minimum(i + 1, n - 1), 0, 0), memory_space=pltpu.SMEM)
    return pl.pallas_call(
        _gather_rows_kernel,
        grid=(n,),
        in_specs=[_index_spec(tm), next_spec, pl.BlockSpec(memory_space=pl.ANY)],
        out_specs=pl.BlockSpec((tm, D), lambda i: (i, 0)),
        scratch_shapes=[pltpu.VMEM((2, tm * SLAB, LANE), F32), pltpu.SemaphoreType.DMA((2,))],
        out_shape=jax.ShapeDtypeStruct((T, D), F32),
        compiler_params=pltpu.CompilerParams(dimension_semantics=("arbitrary",),
                                             vmem_limit_bytes=VMEM_LIMIT),
        name="gather_rows",
    )(idx, idx, ys)


def _moe_kernel(lo_ref, hi_ref, valid_ref, xs_ref, gmoe_ref, wrh_ref, br_ref,
                gu_lo_ref, dn_lo_ref, gu_hi_ref, dn_hi_ref, o_ref):
    i = pl.program_id(0)
    ts = o_ref.shape[0] // SLAB

    @pl.when(valid_ref[i] > 0)
    def _():
        x2 = jnp.concatenate([xs_ref[pl.ds(j, ts, stride=SLAB), :] for j in range(SLAB)], axis=1)
        h = (_rms(x2) * gmoe_ref[...]).astype(BF16)
        lt = lax.dot_general(wrh_ref[...], h, (((1,), (1,)), ((), ())),
                             preferred_element_type=F32) + br_ref[...]
        rows = lax.broadcasted_iota(jnp.int32, lt.shape, 0)
        e_lo, e_hi = lo_ref[i], hi_ref[i]
        pick = lambda r: jnp.sum(jnp.where(rows == r, lt, 0.0), axis=0, keepdims=True)
        is_grp = rows < N_GROUPS
        gmax = jnp.max(jnp.where(is_grp, lt, -jnp.inf), axis=0, keepdims=True)
        p_grp = (jnp.exp(pick(e_lo // EXPERTS_PER_GROUP) - gmax)
                 / jnp.sum(jnp.where(is_grp, jnp.exp(lt - gmax), 0.0), axis=0, keepdims=True))
        s_lo, s_hi = pick(N_GROUPS + e_lo), pick(N_GROUPS + e_hi)
        smax = jnp.maximum(s_lo, s_hi)
        x_lo, x_hi = jnp.exp(s_lo - smax), jnp.exp(s_hi - smax)
        w_lo = p_grp * x_lo / (x_lo + x_hi)
        w_hi = p_grp * x_hi / (x_lo + x_hi)
        rw = lax.broadcasted_iota(jnp.int32, (LANE, ts), 0)
        gates = jnp.where(rw == 0, w_lo, jnp.where(rw == 1, w_hi, 0.0)).T
        y = x2
        for col, gu_ref, dn_ref in ((0, gu_lo_ref, dn_lo_ref), (1, gu_hi_ref, dn_hi_ref)):
            gu = jnp.dot(h, gu_ref[0], preferred_element_type=F32)
            act = jax.nn.silu(gu[:, :D_EXPERT]) * gu[:, D_EXPERT:]
            y = y + gates[:, col:col + 1] * jnp.dot(act.astype(BF16), dn_ref[0], preferred_element_type=F32)
        for j in range(SLAB):
            o_ref[pl.ds(j, ts, stride=SLAB), :] = y[:, j * LANE:(j + 1) * LANE]

    @pl.when(valid_ref[i] == 0)
    def _():
        o_ref[...] = jnp.zeros_like(o_ref)


def _moe(xs, tile_lo, tile_hi, tile_valid, gmoe, wr_hi, br, w_gu_bf, w_dn_bf, ts):
    n_tiles = tile_lo.shape[0]
    D = D_MODEL
    const = lambda shape: pl.BlockSpec(shape, lambda i, lo, hi, va: (0,) * len(shape))
    return pl.pallas_call(
        _moe_kernel,
        grid_spec=pltpu.PrefetchScalarGridSpec(
            num_scalar_prefetch=3,
            grid=(n_tiles,),
            in_specs=[
                pl.BlockSpec((ts * SLAB, LANE), lambda i, lo, hi, va: (i, 0)),
                const((1, D)), const((ROUTE_ROWS, D)), const((ROUTE_ROWS, 1)),
                pl.BlockSpec((1, D, 2 * D_EXPERT), lambda i, lo, hi, va: (lo[i], 0, 0)),
                pl.BlockSpec((1, D_EXPERT, D), lambda i, lo, hi, va: (lo[i], 0, 0)),
                pl.BlockSpec((1, D, 2 * D_EXPERT), lambda i, lo, hi, va: (hi[i], 0, 0)),
                pl.BlockSpec((1, D_EXPERT, D), lambda i, lo, hi, va: (hi[i], 0, 0)),
            ],
            out_specs=pl.BlockSpec((ts * SLAB, LANE), lambda i, lo, hi, va: (i, 0)),
        ),
        out_shape=jax.ShapeDtypeStruct((n_tiles * ts * SLAB, LANE), F32),
        compiler_params=pltpu.CompilerParams(dimension_semantics=("arbitrary",),
                                             vmem_limit_bytes=VMEM_LIMIT),
        name="moe_experts",
    )(tile_lo, tile_hi, tile_valid, xs, gmoe, wr_hi, br, w_gu_bf, w_dn_bf, w_gu_bf, w_dn_bf)


def _route_tables(route, counts, tm, ts):
    T = route.shape[1]
    cls = route[0].astype(jnp.int32)
    rank = route[1].astype(jnp.int32)
    cnt = counts[:, :N_CLASSES, LANE - 1].astype(jnp.int32)
    total = jnp.sum(cnt, axis=0)
    tiles_c = (total + ts - 1) // ts
    tile_start = jnp.cumsum(tiles_c) - tiles_c
    before = jnp.cumsum(cnt, axis=0) - cnt
    base = tile_start[None, :] * ts + before
    classes = jnp.arange(N_CLASSES, dtype=jnp.int32)
    pick = cls.reshape(-1, tm)[:, :, None] == classes[None, None, :]
    dest = jnp.sum(jnp.where(pick, base[:, None, :], 0), axis=-1).reshape(T) + rank
    n_tiles = T // ts + N_CLASSES
    tile_id = jnp.arange(n_tiles, dtype=jnp.int32)
    used = jnp.sum(tiles_c)
    ends = jnp.cumsum(tiles_c)
    tcls = jnp.sum(jnp.minimum(tile_id, used - 1)[:, None] >= ends[None, :], axis=1).astype(jnp.int32)
    tcls = jnp.minimum(tcls, N_CLASSES - 1)
    grp, pr = tcls // len(PAIRS), tcls % len(PAIRS)
    pair_lo = (pr >= 3).astype(jnp.int32) + (pr >= 5).astype(jnp.int32)
    pair_hi = jnp.where(pr < 3, pr + 1, jnp.where(pr < 5, pr - 1, 3))
    tile_lo = grp * EXPERTS_PER_GROUP + pair_lo
    tile_hi = grp * EXPERTS_PER_GROUP + pair_hi
    return dest, tile_lo, tile_hi, (tile_id < used).astype(jnp.int32)


def _moe_block(x2s, route, counts, gmoe, wr_hi, br, w_gu_bf, w_dn_bf, tm):
    T = route.shape[1]
    ts = _pick_tile(T, 256)
    dest, tile_lo, tile_hi, tile_valid = _route_tables(route, counts, tm, ts)
    n_tiles = T // ts + N_CLASSES
    xs = _scatter_rows(x2s, dest, n_tiles * ts, _pick_tile(T, 1024))
    ys = _moe(xs, tile_lo, tile_hi, tile_valid, gmoe, wr_hi, br, w_gu_bf, w_dn_bf, ts)
    return _gather_rows(ys, dest, tm)


def _pick_tile(n, pref):
    t = min(pref, n)
    while n % t:
        t //= 2
    return t


def kernel(x, mem, norm_mix, w_in, hyena_short_w, hyena_short_b, filt_w1, filt_b1, filt_w2, filt_b2, filt_w3, filt_sin_freq, hyena_skip, attn_q_norm, attn_k_norm, out_norm_hyena, out_norm_attn, w_out, norm_xattn, norm_mem, w_q_mem, w_k_mem, w_v_mem, mem_q_norm, mem_k_norm, w_o_mem, norm_moe, w_router_grp, b_router_grp, w_router_exp, b_router_exp, w_gate, w_up, w_down):
    B, L, D = x.shape
    T = B * L
    NR = L // GRID_W
    depth = norm_mix.shape[0]
    tm = _pick_tile(L, 1024)
    tq = _pick_tile(L, 512)
    tk = _pick_tile(L, 512)
    cos_t, sin_t = _rope_tables(L)

    def to_column_major(a):
        return a.reshape(B, NR, GRID_W, -1).transpose(0, 2, 1, 3).reshape(T, -1)

    def to_row_major(a):
        return a.reshape(B, GRID_W, NR, -1).transpose(0, 2, 1, 3).reshape(T, -1)

    for i in range(depth):
        qkg = jnp.concatenate([jnp.tile(attn_q_norm[i], N_Q_HEADS), jnp.tile(attn_k_norm[i], N_KV_HEADS)])[None, :]
        u, q, k, v = _inproj(x, norm_mix[i][None, :], w_in[i].astype(BF16), cos_t, sin_t, qkg, tm)
        y_hy = to_row_major(_hyena(to_column_major(u), hyena_short_w[i], hyena_short_b[i], hyena_skip[i],
                                   filt_w1[i], filt_b1[i], filt_w2[i], filt_b2[i], filt_w3[i],
                                   filt_sin_freq[i], B, L))
        y_at = _attention(q, k, v, out_norm_attn[i][None, :], tq, tk)
        km, vm = _memkv(mem, norm_mem[i][None, :], w_k_mem[i].astype(BF16), w_v_mem[i].astype(BF16),
                        mem_k_norm[i][None, :])
        wr = jnp.concatenate([w_router_grp[i], w_router_exp[i]], axis=1).T
        wr = jnp.pad(wr, ((0, ROUTE_ROWS - wr.shape[0]), (0, 0)))
        wr_hi, wr_lo = _split_bf16(wr)
        br = jnp.pad(jnp.concatenate([b_router_grp[i], b_router_exp[i]]), (0, ROUTE_ROWS - N_GROUPS - N_EXPERTS))
        gmoe = norm_moe[i][None, :]
        x2s, route, counts = _mix(x, y_hy, y_at, out_norm_hyena[i][None, :], w_out[i].astype(BF16),
                                  norm_xattn[i][None, :], w_q_mem[i].astype(BF16), mem_q_norm[i][None, :],
                                  km, vm, w_o_mem[i].astype(BF16), gmoe, wr_hi, wr_lo, br[:, None], tm)
        w_gu = jnp.concatenate([w_gate[i], w_up[i]], axis=-1).astype(BF16)
        x = _moe_block(x2s, route, counts, gmoe, wr_hi, br[:, None], w_gu, w_down[i].astype(BF16),
                       tm).reshape(B, L, D)
    return x
```

```python
import functools
import math

import jax
import jax.numpy as jnp
import numpy as np
from jax import lax
from jax.experimental import pallas as pl
from jax.experimental.pallas import tpu as pltpu

F32 = jnp.float32
BF16 = jnp.bfloat16

D_MODEL = 1024
GRID_W = 64
D_HYENA = 512
HYENA_GROUPS = 8
HYENA_ORDER = 2
SHORT_CONV = 3
FILTER_BANDS = 16
FILTER_FF = 64
DECAY_TARGET = 1e-2
FAST_DECAY_PCT = 0.3
SLOW_DECAY_PCT = 1.5
N_Q_HEADS = 8
N_KV_HEADS = 2
HEAD_DIM = 64
D_ATTN = N_Q_HEADS * HEAD_DIM
D_KV = N_KV_HEADS * HEAD_DIM
ROPE_THETA = 10000.0
D_U = (HYENA_ORDER + 1) * D_HYENA
D_IN = D_U + D_ATTN + 2 * D_KV
D_QK = D_ATTN + D_KV
MEM_HEADS = 4
MEM_HEAD_DIM = 128
D_MEM_ATTN = MEM_HEADS * MEM_HEAD_DIM
N_GROUPS = 4
EXPERTS_PER_GROUP = 4
N_EXPERTS = N_GROUPS * EXPERTS_PER_GROUP
D_EXPERT = 512
EPS = 1e-6

PAIRS = ((0, 1), (0, 2), (0, 3), (1, 2), (1, 3), (2, 3))
N_CLASSES = N_GROUPS * len(PAIRS)
ROUTE_ROWS = 32
LANE = 128
SUBLANE = 8
SLAB = SUBLANE
VMEM_LIMIT = 56 * 1024 * 1024
HY_CB = 256
HY_PAD = 8
HY_UNROLL = 4
DMA_UNROLL = 8
LOG2E = math.log2(math.e)


def _rms(x):
    return x * lax.rsqrt(jnp.mean(x * x, axis=-1, keepdims=True) + EPS)


def _group_sumsq_matrix(width, group):
    idx = np.arange(width) // group
    return jnp.asarray((idx[:, None] == idx[None, :]).astype(np.float32) / group, dtype=BF16)


def _hy_pitch(n1):
    return 4 * n1 + HY_PAD


def _inproj_kernel(x_ref, g_ref, w_ref, cos_ref, sin_ref, qkg_ref, gmat_ref,
                   u_ref, q_ref, k_ref, v_ref):
    x = x_ref[...]
    tm = x.shape[0]
    h = _rms(x) * g_ref[...]
    p = jnp.dot(h.astype(BF16), w_ref[...], preferred_element_type=F32)
    u_ref[...] = p[:, :D_U].astype(BF16)
    qk = p[:, D_U:D_U + D_QK]
    ms = jnp.dot((qk * qk).astype(BF16), gmat_ref[...], preferred_element_type=F32)
    qk = qk * lax.rsqrt(ms + EPS) * qkg_ref[...]
    reps = D_QK // LANE
    cos = jnp.concatenate([cos_ref[...]] * reps, axis=1)
    sin = jnp.concatenate([sin_ref[...]] * reps, axis=1)
    lane = lax.broadcasted_iota(jnp.int32, (tm, D_QK), 1)
    partner = jnp.where((lane % 32) < 16, pltpu.roll(qk, D_QK - 16, 1), pltpu.roll(qk, 16, 1))
    qk = qk * cos + partner * sin
    for hd in range(N_Q_HEADS):
        q_ref[0, hd] = (qk[:, hd * HEAD_DIM:(hd + 1) * HEAD_DIM] * (HEAD_DIM ** -0.5 * LOG2E)).astype(BF16)
    for hd in range(N_KV_HEADS):
        k_ref[0, hd] = qk[:, D_ATTN + hd * HEAD_DIM:D_ATTN + (hd + 1) * HEAD_DIM].astype(BF16)
    vv = p[:, D_U + D_QK:]
    lane_v = lax.broadcasted_iota(jnp.int32, (tm, LANE), 1)
    ones_col = jnp.where(lane_v == HEAD_DIM, 1.0, 0.0)
    v_ref[0, 0] = jnp.where(lane_v < HEAD_DIM, vv, ones_col).astype(BF16)
    v_ref[0, 1] = jnp.where(lane_v < HEAD_DIM, pltpu.roll(vv, HEAD_DIM, 1), ones_col).astype(BF16)


def _inproj(x, g, w_bf, cos_t, sin_t, qkg, tm):
    B, L, D = x.shape
    nt = L // tm
    gmat = _group_sumsq_matrix(D_QK, HEAD_DIM)
    return pl.pallas_call(
        _inproj_kernel,
        grid=(B * nt,),
        in_specs=[
            pl.BlockSpec((tm, D), lambda i: (i, 0)),
            pl.BlockSpec((1, D), lambda i: (0, 0)),
            pl.BlockSpec((D, D_IN), lambda i: (0, 0)),
            pl.BlockSpec((tm, LANE), lambda i: (i % nt, 0)),
            pl.BlockSpec((tm, LANE), lambda i: (i % nt, 0)),
            pl.BlockSpec((1, D_QK), lambda i: (0, 0)),
            pl.BlockSpec((D_QK, D_QK), lambda i: (0, 0)),
        ],
        out_specs=[
            pl.BlockSpec((tm, D_U), lambda i: (i, 0)),
            pl.BlockSpec((1, N_Q_HEADS, tm, HEAD_DIM), lambda i: (i // nt, 0, i % nt, 0)),
            pl.BlockSpec((1, N_KV_HEADS, tm, HEAD_DIM), lambda i: (i // nt, 0, i % nt, 0)),
            pl.BlockSpec((1, N_KV_HEADS, tm, LANE), lambda i: (i // nt, 0, i % nt, 0)),
        ],
        out_shape=[
            jax.ShapeDtypeStruct((B * L, D_U), BF16),
            jax.ShapeDtypeStruct((B, N_Q_HEADS, L, HEAD_DIM), BF16),
            jax.ShapeDtypeStruct((B, N_KV_HEADS, L, HEAD_DIM), BF16),
            jax.ShapeDtypeStruct((B, N_KV_HEADS, L, LANE), BF16),
        ],
        compiler_params=pltpu.CompilerParams(dimension_semantics=("parallel",),
                                             vmem_limit_bytes=VMEM_LIMIT),
        name="inproj",
    )(x.reshape(B * L, D), g, w_bf, cos_t, sin_t, qkg, gmat)


def _rope_tables(L):
    p = np.arange(L)
    row, col = p // GRID_W, p % GRID_W
    d = np.arange(LANE) % HEAD_DIM
    sec, r = d // 32, d % 32
    inv = (ROPE_THETA ** (-(r % 16).astype(np.float64) / 16.0)).astype(np.float32)
    pos = np.where(sec[None, :] == 0, row[:, None], col[:, None]).astype(np.float32)
    ang = pos * inv[None, :]
    sign = np.where(r < 16, -1.0, 1.0)[None, :]
    return jnp.asarray(np.cos(ang), F32), jnp.asarray(np.sin(ang) * sign, F32)


def _real_embed(m):
    return np.block([[m.real, -m.imag], [m.imag, m.real]])


@functools.lru_cache(maxsize=None)
def _dft_tables(n1, n2):
    L = n1 * n2
    i1, i2 = np.arange(n1), np.arange(n2)
    om = np.exp(-1j * np.pi / L)
    f1 = np.exp(-2j * np.pi * np.outer(i1, i1) / n1)
    tw = np.exp(-2j * np.pi * np.outer(i1, i2) / L)
    fwd1, fwd1_real = [], []
    for t in i2:
        ev = tw[:, t][:, None] * f1
        od = (om ** t) * tw[:, t][:, None] * f1 * (om ** (n2 * i1))[None, :]
        fwd1.append(np.concatenate([_real_embed(ev), _real_embed(od)], axis=0))
        z = np.zeros((n1, n1))
        fwd1_real.append(np.block([[ev.real, z], [ev.imag, z], [z, od.real], [z, od.imag]]))
    f2 = _real_embed(np.exp(-2j * np.pi * np.outer(i2, i2) / n2))
    f2i = np.exp(2j * np.pi * np.outer(i2, i2) / n2)
    inv_e = np.stack([_real_embed(np.conj(tw[k, :])[:, None] * f2i) for k in i1])
    inv_o = np.stack([_real_embed((om ** (-i2))[:, None] * np.conj(tw[k, :])[:, None] * f2i) for k in i1])
    f1i = np.exp(2j * np.pi * np.outer(i1, i1) / n1)
    inv2 = np.concatenate([_real_embed(f1i), _real_embed((om ** (-n2 * i1))[:, None] * f1i)], axis=1) / (2 * L)
    as_f32 = lambda a: np.asarray(a, np.float32)
    return {"fwd1": as_f32(np.stack(fwd1)), "fwd1_real": as_f32(np.stack(fwd1_real)), "fwd2": as_f32(f2),
            "inv_e": as_f32(inv_e), "inv_o": as_f32(inv_o), "inv2": as_f32(inv2)}


def _aligned(start, align):
    return start if isinstance(start, int) else pl.multiple_of(start, align)


def _conv_block(ref, w_ref, b_ref, base, t2, n1, n2, edge):
    def blk(i):
        return ref[pl.ds(_aligned(base + i * n1, n1), n1), :].astype(F32)

    cur = blk(t2)
    if not edge:
        prev, nxt = blk(jnp.maximum(t2 - 1, 0)), blk(jnp.minimum(t2 + 1, n2 - 1))
    else:
        row = lax.broadcasted_iota(jnp.int32, cur.shape, 0)
        if t2 == 0:
            prev = jnp.where(row == 0, 0.0, pltpu.roll(blk(n2 - 1), 1, 0))
            nxt = blk(1)
        else:
            prev = blk(n2 - 2)
            nxt = jnp.where(row == n1 - 1, 0.0, pltpu.roll(blk(0), n1 - 1, 0))
    return prev * w_ref[0:1, :] + cur * w_ref[1:2, :] + nxt * w_ref[2:3, :] + b_ref[...]


def _plain_block(ref, base, t2, n1):
    return ref[pl.ds(_aligned(base + t2 * n1, n1), n1), :].astype(F32)


def _store_slabs(a_ref, rows, val):
    for s in range(val.shape[1] // LANE):
        a_ref[s, rows, :] = val[:, s * LANE:(s + 1) * LANE]


def _load_slabs(a_ref, rows):
    return jnp.concatenate([a_ref[s, rows, :] for s in range(a_ref.shape[0])], axis=1)


def _forward_stage1(load_stacked, fwd1_ref, a_ref, n1, n2, redo_edges):
    pitch = _hy_pitch(n1)

    def body(t2, edge):
        a = jnp.dot(fwd1_ref[t2], load_stacked(t2, edge).astype(BF16), preferred_element_type=F32)
        _store_slabs(a_ref, pl.ds(_aligned(t2 * pitch, SUBLANE), 4 * n1), a)

    def loop_body(t2, carry):
        body(t2, False)
        return carry

    lax.fori_loop(0, n2, loop_body, 0, unroll=HY_UNROLL)
    if redo_edges:
        body(0, True)
        body(n2 - 1, True)


def _forward_stage2(a_ref, fwd2_ref, k1s, n1, n2):
    pitch = _hy_pitch(n1)
    cols = []
    for k1 in k1s:
        for half in range(2):
            cols.append(jnp.concatenate(
                [_load_slabs(a_ref, pl.ds((2 * half + part) * n1 + k1, n2, stride=pitch)) for part in range(2)],
                axis=0))
    x = jnp.dot(fwd2_ref[...], jnp.concatenate(cols, axis=1).astype(BF16), preferred_element_type=F32)
    w2 = x.shape[1] // len(k1s)
    return [(x[:n2, i * w2:(i + 1) * w2], x[n2:, i * w2:(i + 1) * w2]) for i in range(len(k1s))]


def _hyena_kernel(*refs, n1, n2, conv_z):
    L = n1 * n2
    if conv_z:
        (z_ref, zw_ref, zb_ref, g_ref, gw_ref, gb_ref, ker_ref, kei_ref, kor_ref, koi_ref,
         skip_ref, fwd1_ref, fwd2_ref, inve_ref, invo_ref, inv2_ref, o_ref, a_ref) = refs
        load_z = lambda part, t2, edge: _conv_block(z_ref, zw_ref, zb_ref, part * L, t2, n1, n2, edge)
    else:
        (z_ref, g_ref, gw_ref, gb_ref, ker_ref, kei_ref, kor_ref, koi_ref,
         skip_ref, fwd1_ref, fwd2_ref, inve_ref, invo_ref, inv2_ref, o_ref, a_ref) = refs
        load_z = lambda part, t2, edge: _plain_block(z_ref, part * L, t2, n1)
    cb = z_ref.shape[1]
    pitch = _hy_pitch(n1)

    _forward_stage1(lambda t2, edge: jnp.concatenate([load_z(0, t2, edge), load_z(1, t2, edge)], axis=0),
                    fwd1_ref, a_ref, n1, n2, redo_edges=conv_z)

    skip2 = jnp.concatenate([skip_ref[...], skip_ref[...]], axis=1)
    group = HY_UNROLL if n1 % HY_UNROLL == 0 else 1

    def freq_body(i, carry):
        k1s = [i * group + j for j in range(group)]
        results = []
        for k1, (xr, xi) in zip(k1s, _forward_stage2(a_ref, fwd2_ref, k1s, n1, n2)):
            rows = pl.ds(pl.multiple_of(k1 * n2, n2), n2)
            kr = jnp.concatenate([ker_ref[rows, :], kor_ref[rows, :]], axis=1).astype(F32) + skip2
            ki = jnp.concatenate([kei_ref[rows, :], koi_ref[rows, :]], axis=1).astype(F32)
            yr = xr * kr - xi * ki
            yi = xr * ki + xi * kr
            ye = jnp.concatenate([yr[:, :cb], yi[:, :cb]], axis=0).astype(BF16)
            yo = jnp.concatenate([yr[:, cb:], yi[:, cb:]], axis=0).astype(BF16)
            be = jnp.dot(inve_ref[k1], ye, preferred_element_type=F32)
            bo = jnp.dot(invo_ref[k1], yo, preferred_element_type=F32)
            results.append((be[:n2], be[n2:], bo[:n2], bo[n2:]))
        for k1, quarters in zip(k1s, results):
            for q, val in enumerate(quarters):
                _store_slabs(a_ref, pl.ds(q * n1 + k1, n2, stride=pitch), val)
        return carry

    lax.fori_loop(0, n1 // group, freq_body, 0)

    def time_block(t2, edge):
        blk = _load_slabs(a_ref, pl.ds(_aligned(t2 * pitch, SUBLANE), 4 * n1))
        y = jnp.dot(inv2_ref[...], blk.astype(BF16), preferred_element_type=F32)
        for part in range(2):
            gate = _conv_block(g_ref, gw_ref, gb_ref, part * L, t2, n1, n2, edge)
            o_ref[pl.ds(_aligned(part * L + t2 * n1, n1), n1), :] = (
                gate * y[part * n1:(part + 1) * n1]).astype(o_ref.dtype)

    def time_body(t2, carry):
        time_block(t2, False)
        return carry

    lax.fori_loop(0, n2, time_body, 0, unroll=HY_UNROLL)
    time_block(0, True)
    time_block(n2 - 1, True)


def _hyena_order(z, z_col0, u, gate_col0, short_w, short_b, spectra, spec_col0, skip, tabs, B, L, conv_z):
    n2 = GRID_W
    n1 = L // n2
    cb = HY_CB
    ncb = D_HYENA // cb
    pitch = _hy_pitch(n1)
    col = lambda off: (lambda c, p: (0, off + c))
    pair = lambda off: (lambda c, p: (p, off + c))
    one = pl.Buffered(1)
    const3 = lambda a: pl.BlockSpec(a.shape, lambda c, p: (0, 0, 0))
    const2 = lambda a: pl.BlockSpec(a.shape, lambda c, p: (0, 0))
    in_specs = [pl.BlockSpec((2 * L, cb), pair(z_col0), pipeline_mode=one)]
    args = [z]
    if conv_z:
        in_specs += [pl.BlockSpec((SHORT_CONV, cb), col(z_col0)), pl.BlockSpec((1, cb), col(z_col0))]
        args += [short_w, short_b]
    in_specs += [pl.BlockSpec((2 * L, cb), pair(gate_col0), pipeline_mode=one),
                 pl.BlockSpec((SHORT_CONV, cb), col(gate_col0)), pl.BlockSpec((1, cb), col(gate_col0))]
    args += [u, short_w, short_b]
    in_specs += [pl.BlockSpec((L, cb), col(spec_col0), pipeline_mode=one)] * 4
    args += list(spectra)
    in_specs += [pl.BlockSpec((1, cb), lambda c, p: (0, c)),
                 const3(tabs["fwd1"]), const2(tabs["fwd2"]), const3(tabs["inv_e"]), const3(tabs["inv_o"]),
                 const2(tabs["inv2"])]
    args += [skip, tabs["fwd1"], tabs["fwd2"], tabs["inv_e"], tabs["inv_o"], tabs["inv2"]]
    return pl.pallas_call(
        functools.partial(_hyena_kernel, n1=n1, n2=n2, conv_z=conv_z),
        grid=(ncb, B // 2),
        in_specs=in_specs,
        out_specs=pl.BlockSpec((2 * L, cb), lambda c, p: (p, c)),
        out_shape=jax.ShapeDtypeStruct((B * L, D_HYENA), BF16),
        scratch_shapes=[pltpu.VMEM((cb // LANE, n2 * pitch, LANE), F32)],
        compiler_params=pltpu.CompilerParams(dimension_semantics=("arbitrary", "arbitrary"),
                                             vmem_limit_bytes=VMEM_LIMIT),
        name="hyena_order",
    )(*args)


def _spectra_kernel(kc_ref, kn_ref, fwd1_ref, fwd2_ref, er_ref, ei_ref, or_ref, oi_ref, a_ref, *, n1, n2):
    cb = kc_ref.shape[1]
    _forward_stage1(lambda t2, edge: jnp.concatenate([_plain_block(kc_ref, 0, t2, n1),
                                                     _plain_block(kn_ref, 0, t2, n1)], axis=0),
                    fwd1_ref, a_ref, n1, n2, redo_edges=False)

    def freq_body(k1, carry):
        (xr, xi), = _forward_stage2(a_ref, fwd2_ref, [k1], n1, n2)
        rows = pl.ds(pl.multiple_of(k1 * n2, n2), n2)
        er_ref[rows, :] = xr[:, :cb].astype(BF16)
        ei_ref[rows, :] = xi[:, :cb].astype(BF16)
        or_ref[rows, :] = xr[:, cb:].astype(BF16)
        oi_ref[rows, :] = xi[:, cb:].astype(BF16)
        return carry

    lax.fori_loop(0, n1, freq_body, 0, unroll=HY_UNROLL)


def _filter_spectra(kc, kn, tabs, L):
    n2 = GRID_W
    n1 = L // n2
    cb = HY_CB
    C = kc.shape[1]
    pitch = _hy_pitch(n1)
    return pl.pallas_call(
        functools.partial(_spectra_kernel, n1=n1, n2=n2),
        grid=(C // cb,),
        in_specs=[pl.BlockSpec((L, cb), lambda c: (0, c)), pl.BlockSpec((L, cb), lambda c: (0, c)),
                  pl.BlockSpec(tabs["fwd1_real"].shape, lambda c: (0, 0, 0)),
                  pl.BlockSpec(tabs["fwd2"].shape, lambda c: (0, 0))],
        out_specs=[pl.BlockSpec((L, cb), lambda c: (0, c))] * 4,
        out_shape=[jax.ShapeDtypeStruct((L, C), BF16)] * 4,
        scratch_shapes=[pltpu.VMEM((cb // LANE, n2 * pitch, LANE), F32)],
        compiler_params=pltpu.CompilerParams(dimension_semantics=("arbitrary",),
                                             vmem_limit_bytes=VMEM_LIMIT),
        name="filter_spectra",
    )(kc, kn, tabs["fwd1_real"], tabs["fwd2"])


def _filter_time_kernel(zf_ref, zr_ref, w1_ref, b1_ref, w2_ref, b2_ref, w3f_ref, w3b_ref, sf_ref,
                        dl_ref, kc_ref, kn_ref, hf_ref, hr_ref):
    hp = lax.Precision.HIGHEST

    def dot3(a, b):
        a_hi, a_lo = _split_bf16(a)
        b_hi, b_lo = _split_bf16(b)
        return (jnp.dot(a_hi, b_hi, preferred_element_type=F32) + jnp.dot(a_hi, b_lo, preferred_element_type=F32)
                + jnp.dot(a_lo, b_hi, preferred_element_type=F32))

    df = jnp.exp(-zf_ref[:, 0:1] * dl_ref[...])
    row = lax.broadcasted_iota(jnp.int32, df.shape, 0)
    dr = jnp.where(row > 0, jnp.exp(-zr_ref[:, 0:1] * dl_ref[...]), 0.0)

    @pl.when(pl.program_id(0) == 0)
    def _():
        for z_ref, h_ref in ((zf_ref, hf_ref), (zr_ref, hr_ref)):
            a = jnp.sin(sf_ref[0:1, :] * (jnp.dot(z_ref[...], w1_ref[...], precision=hp,
                                                  preferred_element_type=F32) + b1_ref[...]))
            h_ref[...] = jnp.sin(sf_ref[1:2, :] * (jnp.dot(a, w2_ref[...], precision=hp,
                                                           preferred_element_type=F32) + b2_ref[...]))

    fwd = dot3(hf_ref[...], w3f_ref[...]) * df
    bwd = dot3(hr_ref[...], w3b_ref[...]) * dr
    norm = (jnp.sum(jnp.abs(fwd), axis=0, keepdims=True) + jnp.sum(jnp.abs(bwd), axis=0, keepdims=True)
            + EPS)
    kc_ref[...] = (fwd + bwd) / norm
    kn_ref[...] = (fwd - bwd) / norm


def _filter_time(w1, b1, w2, b2, w3, sin_freq, L):
    NR = L // GRID_W
    f64 = np.float64
    p = np.arange(L)
    t_nat = (p % NR) * GRID_W + p // NR
    tl = np.linspace(0.0, 1.0, L)
    bands = np.linspace(1e-4, FILTER_BANDS - 1, FILTER_BANDS)
    wv = (2.0 * math.pi / L) * np.arange(L, dtype=f64)
    feats = np.concatenate([tl[:, None], np.cos(bands[None, :] * wv[:, None]),
                            np.sin(bands[None, :] * wv[:, None])], axis=1)
    deltas = np.abs(np.linspace(math.log(DECAY_TARGET) / SLOW_DECAY_PCT,
                                math.log(DECAY_TARGET) / FAST_DECAY_PCT, D_HYENA))
    t_rev = (L - t_nat) % L
    pad = LANE - feats.shape[1]
    zf = np.pad(feats[t_nat], ((0, 0), (0, pad)))
    zr = np.pad(feats[t_rev], ((0, 0), (0, pad)))
    dl = np.tile(deltas, HYENA_ORDER)[None, :]
    w1p = jnp.pad(w1, ((0, pad), (0, 0)))
    w3r = w3.reshape(FILTER_FF, HYENA_ORDER, 2, D_HYENA)
    w3f = w3r[:, :, 0].reshape(FILTER_FF, HYENA_ORDER * D_HYENA)
    w3b = w3r[:, :, 1].reshape(FILTER_FF, HYENA_ORDER * D_HYENA)
    C = HYENA_ORDER * D_HYENA
    cb = HY_CB
    full = lambda a: pl.BlockSpec(a.shape, lambda c: (0,) * a.ndim)
    colb = lambda r: pl.BlockSpec((r, cb), lambda c: (0, c))
    zf, zr, dl = (jnp.asarray(a, F32) for a in (zf, zr, dl))
    args = (zf, zr, w1p, b1[None, :], w2, b2[None, :], w3f, w3b, sin_freq, dl)
    return pl.pallas_call(
        _filter_time_kernel,
        grid=(C // cb,),
        in_specs=[full(zf), full(zr), full(w1p), full(args[3]), full(w2), full(args[5]),
                  colb(FILTER_FF), colb(FILTER_FF), full(sin_freq), colb(1)],
        out_specs=[colb(L), colb(L)],
        out_shape=[jax.ShapeDtypeStruct((L, C), F32)] * 2,
        scratch_shapes=[pltpu.VMEM((L, FILTER_FF), F32)] * 2,
        compiler_params=pltpu.CompilerParams(dimension_semantics=("arbitrary",),
                                             vmem_limit_bytes=VMEM_LIMIT),
        name="filter_time",
    )(*args)


def _hyena(u, short_w, short_b, skip, w1, b1, w2, b2, w3, sin_freq, B, L):
    tabs = {name: jnp.asarray(t).astype(BF16) for name, t in _dft_tables(L // GRID_W, GRID_W).items()}
    kc, kn = _filter_time(w1, b1, w2, b2, w3, sin_freq, L)
    spectra = _filter_spectra(kc, kn, tabs, L)
    ncb = D_HYENA // HY_CB
    sb = short_b[None, :]
    z1 = _hyena_order(u, 0, u, ncb, short_w, sb, spectra, 0, skip[0:1], tabs, B, L, True)
    return _hyena_order(z1, 0, u, 2 * ncb, short_w, sb, spectra, ncb, skip[1:2], tabs, B, L, False)


def _attn_kernel(q_ref, k_ref, v_ref, g_ref, st_ref, o_ref, *, tk):
    G, tq = q_ref.shape[1], q_ref.shape[2]
    L = k_ref.shape[2]
    R = G * tq
    q = q_ref[0].reshape(R, HEAD_DIM)
    m = jnp.full((R, 1), -jnp.inf, F32)
    acc = jnp.zeros((R, LANE), F32)
    for c in range(L // tk):
        kc = k_ref[0, 0, c * tk:(c + 1) * tk, :]
        s = lax.dot_general(q, kc, (((1,), (1,)), ((), ())), preferred_element_type=F32)
        m_new = jnp.maximum(m, jnp.max(s, axis=-1, keepdims=True))
        p = jnp.exp2((s - m_new).astype(BF16))
        acc = jnp.exp2(m - m_new) * acc + jnp.dot(p, v_ref[0, 0, c * tk:(c + 1) * tk, :],
                                                   preferred_element_type=F32)
        m = m_new
    a_hi, a_lo = _split_bf16(acc * acc)
    ms = (jnp.dot(a_hi, st_ref[:, :LANE], preferred_element_type=F32)
          + jnp.dot(a_lo, st_ref[:, :LANE], preferred_element_type=F32))
    l_hi, l_lo = _split_bf16(acc)
    l = (jnp.dot(l_hi, st_ref[:, LANE:], preferred_element_type=F32)
         + jnp.dot(l_lo, st_ref[:, LANE:], preferred_element_type=F32))
    o = (acc * lax.rsqrt(ms + EPS * l * l))[:, :HEAD_DIM]
    o = jnp.concatenate([o[h * tq:(h + 1) * tq] for h in range(G)], axis=1)
    o_ref[0] = (o * g_ref[...]).astype(o_ref.dtype)


def _attention(q, k, v, g_out, tq, tk):
    B, H, L, _ = q.shape
    G = H // N_KV_HEADS
    st = np.zeros((LANE, 2 * LANE), np.float32)
    st[:HEAD_DIM, :LANE] = 1.0 / HEAD_DIM
    st[HEAD_DIM, LANE:] = 1.0
    st = jnp.asarray(st, dtype=BF16)
    return pl.pallas_call(
        functools.partial(_attn_kernel, tk=tk),
        grid=(B, N_KV_HEADS, L // tq),
        in_specs=[
            pl.BlockSpec((1, G, tq, HEAD_DIM), lambda b, kv, i: (b, kv, i, 0)),
            pl.BlockSpec((1, 1, L, HEAD_DIM), lambda b, kv, i: (b, kv, 0, 0)),
            pl.BlockSpec((1, 1, L, LANE), lambda b, kv, i: (b, kv, 0, 0)),
            pl.BlockSpec((1, G * HEAD_DIM), lambda b, kv, i: (0, kv)),
            pl.BlockSpec((LANE, 2 * LANE), lambda b, kv, i: (0, 0)),
        ],
        out_specs=pl.BlockSpec((1, tq, G * HEAD_DIM), lambda b, kv, i: (b, i, kv)),
        out_shape=jax.ShapeDtypeStruct((B, L, D_ATTN), BF16),
        compiler_params=pltpu.CompilerParams(dimension_semantics=("parallel", "parallel", "parallel"),
                                             vmem_limit_bytes=VMEM_LIMIT),
        name="attention",
    )(q, k, v, g_out, st)


def _memkv_kernel(m_ref, g_ref, wk_ref, wv_ref, kg_ref, k_ref, v_ref):
    h = (_rms(m_ref[0]) * g_ref[...]).astype(BF16)
    k = jnp.dot(h, wk_ref[...], preferred_element_type=F32)
    v = jnp.dot(h, wv_ref[...], preferred_element_type=F32)
    kn = [_rms(k[:, i * MEM_HEAD_DIM:(i + 1) * MEM_HEAD_DIM]) * kg_ref[...] for i in range(MEM_HEADS)]
    k_ref[0] = jnp.concatenate(kn, axis=1).astype(BF16)
    v_ref[0] = v.astype(BF16)


def _memkv(mem, g, wk_bf, wv_bf, kg):
    B, M, D = mem.shape
    return pl.pallas_call(
        _memkv_kernel,
        grid=(B,),
        in_specs=[
            pl.BlockSpec((1, M, D), lambda b: (b, 0, 0)),
            pl.BlockSpec((1, D), lambda b: (0, 0)),
            pl.BlockSpec((D, D_MEM_ATTN), lambda b: (0, 0)),
            pl.BlockSpec((D, D_MEM_ATTN), lambda b: (0, 0)),
            pl.BlockSpec((1, MEM_HEAD_DIM), lambda b: (0, 0)),
        ],
        out_specs=[pl.BlockSpec((1, M, D_MEM_ATTN), lambda b: (b, 0, 0))] * 2,
        out_shape=[jax.ShapeDtypeStruct((B, M, D_MEM_ATTN), BF16)] * 2,
        compiler_params=pltpu.CompilerParams(dimension_semantics=("parallel",),
                                             vmem_limit_bytes=VMEM_LIMIT),
        name="memkv",
    )(mem, g, wk_bf, wv_bf, kg)


def _split_bf16(a):
    hi = a.astype(BF16)
    return hi, (a - hi.astype(F32)).astype(BF16)


def _router_logits(h, wrh_ref, wrl_ref, br_ref):
    h_hi, h_lo = _split_bf16(h)
    dn = (((1,), (1,)), ((), ()))
    return (lax.dot_general(wrh_ref[...], h_hi, dn, preferred_element_type=F32)
            + lax.dot_general(wrh_ref[...], h_lo, dn, preferred_element_type=F32)
            + lax.dot_general(wrl_ref[...], h_hi, dn, preferred_element_type=F32)) + br_ref[...]


def _mix_kernel(x_ref, yh_ref, ya_ref, gh_ref, gmat_ref, wo_ref, gx_ref, wq_ref, qg_ref,
                km_ref, vm_ref, wom_ref, gmoe_ref, wrh_ref, wrl_ref, br_ref, tri_ref,
                x2_ref, route_ref, cnt_ref):
    x = x_ref[...]
    tm = x.shape[0]
    yh = yh_ref[...].astype(F32)
    ms = jnp.dot((yh * yh).astype(BF16), gmat_ref[...], preferred_element_type=F32)
    yh = yh * lax.rsqrt(ms + EPS) * gh_ref[...]
    x1 = (x
          + jnp.dot(yh.astype(BF16), wo_ref[:D_HYENA, :], preferred_element_type=F32)
          + jnp.dot(ya_ref[...], wo_ref[D_HYENA:, :], preferred_element_type=F32))
    h2 = (_rms(x1) * gx_ref[...]).astype(BF16)
    qm = jnp.dot(h2, wq_ref[...], preferred_element_type=F32)
    heads = []
    for i in range(MEM_HEADS):
        sl = slice(i * MEM_HEAD_DIM, (i + 1) * MEM_HEAD_DIM)
        qn = (_rms(qm[:, sl]) * qg_ref[...] * (MEM_HEAD_DIM ** -0.5)).astype(BF16)
        s = lax.dot_general(qn, km_ref[0, :, sl], (((1,), (1,)), ((), ())), preferred_element_type=F32)
        p = jnp.exp(s - jnp.max(s, axis=-1, keepdims=True))
        o = jnp.dot(p.astype(BF16), vm_ref[0, :, sl], preferred_element_type=F32)
        heads.append(o / jnp.sum(p, axis=-1, keepdims=True))
    om = jnp.concatenate(heads, axis=1).astype(BF16)
    x2 = x1 + jnp.dot(om, wom_ref[...], preferred_element_type=F32)
    for j in range(SLAB):
        x2_ref[pl.ds(j, tm, stride=SLAB), :] = x2[:, j * LANE:(j + 1) * LANE]

    lt = _router_logits(_rms(x2) * gmoe_ref[...], wrh_ref, wrl_ref, br_ref)
    g = [lt[i:i + 1, :] for i in range(N_GROUPS)]
    gmax = jnp.maximum(jnp.maximum(g[0], g[1]), jnp.maximum(g[2], g[3]))
    gidx = jnp.where(g[0] == gmax, 0, jnp.where(g[1] == gmax, 1, jnp.where(g[2] == gmax, 2, 3)))
    sel = []
    for j in range(EXPERTS_PER_GROUP):
        rows = [lt[N_GROUPS + EXPERTS_PER_GROUP * i + j:N_GROUPS + EXPERTS_PER_GROUP * i + j + 1, :]
                for i in range(N_GROUPS)]
        sel.append(jnp.where(gidx == 0, rows[0], jnp.where(gidx == 1, rows[1],
                                                            jnp.where(gidx == 2, rows[2], rows[3]))))
    def first_argmax(vals):
        best = jnp.maximum(jnp.maximum(vals[0], vals[1]), jnp.maximum(vals[2], vals[3]))
        return jnp.where(vals[0] == best, 0, jnp.where(vals[1] == best, 1, jnp.where(vals[2] == best, 2, 3)))

    a = first_argmax(sel)
    b = first_argmax([jnp.where(a == j, -jnp.inf, sel[j]) for j in range(EXPERTS_PER_GROUP)])
    lo, hi = jnp.minimum(a, b), jnp.maximum(a, b)
    pair = jnp.where(lo == 0, hi - 1, jnp.where(lo == 1, hi + 1, 5))
    cls = gidx * len(PAIRS) + pair

    rows_i = lax.broadcasted_iota(jnp.int32, (ROUTE_ROWS, tm), 0)
    onehot = rows_i == cls
    prefix = jnp.dot(jnp.where(onehot, 1.0, 0.0).astype(BF16), tri_ref[...], preferred_element_type=F32)
    rank = jnp.sum(jnp.where(onehot, prefix, 0.0), axis=0, keepdims=True) - 1.0
    rr = lax.broadcasted_iota(jnp.int32, (8, tm), 0)
    route_ref[...] = jnp.where(rr == 0, cls.astype(F32), jnp.where(rr == 1, rank, 0.0))
    cnt_ref[0] = prefix[:, tm - LANE:]


def _mix(x, y_hy, y_at, gh, wo_bf, gx, wq_bf, qg, km, vm, wom_bf, gmoe, wr_hi, wr_lo, br, tm):
    B, L, D = x.shape
    nt = L // tm
    T = B * L
    M = km.shape[1]
    gmat = _group_sumsq_matrix(D_HYENA, D_HYENA // HYENA_GROUPS)
    tri = jnp.asarray(np.triu(np.ones((tm, tm), np.float32)), dtype=BF16)
    const = lambda shape: pl.BlockSpec(shape, lambda i: (0,) * len(shape))
    return pl.pallas_call(
        _mix_kernel,
        grid=(B * nt,),
        in_specs=[
            pl.BlockSpec((tm, D), lambda i: (i, 0)),
            pl.BlockSpec((tm, D_HYENA), lambda i: (i, 0)),
            pl.BlockSpec((tm, D_ATTN), lambda i: (i, 0)),
            const((1, D_HYENA)), const((D_HYENA, D_HYENA)), const((D_HYENA + D_ATTN, D)),
            const((1, D)), const((D, D_MEM_ATTN)), const((1, MEM_HEAD_DIM)),
            pl.BlockSpec((1, M, D_MEM_ATTN), lambda i: (i // nt, 0, 0)),
            pl.BlockSpec((1, M, D_MEM_ATTN), lambda i: (i // nt, 0, 0)),
            const((D_MEM_ATTN, D)), const((1, D)),
            const((ROUTE_ROWS, D)), const((ROUTE_ROWS, D)), const((ROUTE_ROWS, 1)),
            const((tm, tm)),
        ],
        out_specs=[
            pl.BlockSpec((tm * SLAB, LANE), lambda i: (i, 0)),
            pl.BlockSpec((8, tm), lambda i: (0, i)),
            pl.BlockSpec((1, ROUTE_ROWS, LANE), lambda i: (i, 0, 0)),
        ],
        out_shape=[
            jax.ShapeDtypeStruct((T * SLAB, LANE), F32),
            jax.ShapeDtypeStruct((8, T), F32),
            jax.ShapeDtypeStruct((B * nt, ROUTE_ROWS, LANE), F32),
        ],
        compiler_params=pltpu.CompilerParams(dimension_semantics=("parallel",),
                                             vmem_limit_bytes=VMEM_LIMIT),
        name="mix_mem_route",
    )(x.reshape(T, D), y_hy, y_at.reshape(T, D_ATTN), gh, gmat, wo_bf, gx, wq_bf, qg,
      km, vm, wom_bf, gmoe, wr_hi, wr_lo, br, tri)


def _slab_copy(src_ref, src_tok, dst_ref, dst_tok, n, sem):
    s0 = pl.multiple_of(src_tok * SLAB, SLAB)
    d0 = pl.multiple_of(dst_tok * SLAB, SLAB)
    return pltpu.make_async_copy(src_ref.at[pl.ds(s0, n * SLAB)], dst_ref.at[pl.ds(d0, n * SLAB)], sem)


def _index_spec(rows):
    return pl.BlockSpec((1, 1, rows), lambda i: (i, 0, 0), memory_space=pltpu.SMEM)


def _scatter_rows_kernel(idx_ref, src_ref, init_ref, dst_ref, sem, *, rows):
    del init_ref

    def issue(r, carry):
        _slab_copy(src_ref, r, dst_ref, idx_ref[0, 0, r], 1, sem).start()
        return carry

    lax.fori_loop(0, rows, issue, 0, unroll=DMA_UNROLL)
    _slab_copy(src_ref, 0, dst_ref, 0, rows, sem).wait()


def _scatter_rows(src, dest, n_out, rows):
    T = dest.shape[0]
    init = jnp.zeros((n_out * SLAB, LANE), src.dtype)
    return pl.pallas_call(
        functools.partial(_scatter_rows_kernel, rows=rows),
        grid=(T // rows,),
        in_specs=[_index_spec(rows), pl.BlockSpec((rows * SLAB, LANE), lambda i: (i, 0)),
                  pl.BlockSpec(memory_space=pl.ANY)],
        out_specs=pl.BlockSpec(memory_space=pl.ANY),
        scratch_shapes=[pltpu.SemaphoreType.DMA(())],
        out_shape=jax.ShapeDtypeStruct((n_out * SLAB, LANE), src.dtype),
        input_output_aliases={2: 0},
        compiler_params=pltpu.CompilerParams(dimension_semantics=("arbitrary",),
                                             vmem_limit_bytes=VMEM_LIMIT),
        name="scatter_rows",
    )(dest.reshape(T // rows, 1, rows), src, init)


def _gather_rows_kernel(idx_ref, idx_next_ref, ys_ref, o_ref, buf_ref, sem):
    i = pl.program_id(0)
    tm = o_ref.shape[0]

    def issue(tile_idx_ref, slot):
        def body(r, carry):
            _slab_copy(ys_ref, tile_idx_ref[0, 0, r], buf_ref.at[slot], r, 1, sem.at[slot]).start()
            return carry

        lax.fori_loop(0, tm, body, 0, unroll=DMA_UNROLL)

    def step(slot):
        @pl.when(i == 0)
        def _():
            issue(idx_ref, slot)

        @pl.when(i + 1 < pl.num_programs(0))
        def _():
            issue(idx_next_ref, 1 - slot)

        _slab_copy(ys_ref, 0, buf_ref.at[slot], 0, tm, sem.at[slot]).wait()
        o_ref[...] = jnp.concatenate([buf_ref[slot, pl.ds(j, tm, stride=SLAB), :] for j in range(SLAB)],
                                     axis=1)

    for slot in range(2):
        pl.when(i % 2 == slot)(functools.partial(step, slot))


def _gather_rows(ys, dest, tm):
    T = dest.shape[0]
    D = SLAB * LANE
    n = T // tm
    idx = dest.reshape(n, 1, tm)
    next_spec = pl.BlockSpec((1, 1, tm), lambda i: (jnp.minimum(i + 1, n - 1), 0, 0), memory_space=pltpu.SMEM)
    return pl.pallas_call(
        _gather_rows_kernel,
        grid=(n,),
        in_specs=[_index_spec(tm), next_spec, pl.BlockSpec(memory_space=pl.ANY)],
        out_specs=pl.BlockSpec((tm, D), lambda i: (i, 0)),
        scratch_shapes=[pltpu.VMEM((2, tm * SLAB, LANE), F32), pltpu.SemaphoreType.DMA((2,))],
        out_shape=jax.ShapeDtypeStruct((T, D), F32),
        compiler_params=pltpu.CompilerParams(dimension_semantics=("arbitrary",),
                                             vmem_limit_bytes=VMEM_LIMIT),
        name="gather_rows",
    )(idx, idx, ys)


def _moe_kernel(lo_ref, hi_ref, valid_ref, xs_ref, gmoe_ref, wrh_ref, br_ref,
                gu_lo_ref, dn_lo_ref, gu_hi_ref, dn_hi_ref, o_ref):
    i = pl.program_id(0)
    ts = o_ref.shape[0] // SLAB

    @pl.when(valid_ref[i] > 0)
    def _():
        x2 = jnp.concatenate([xs_ref[pl.ds(j, ts, stride=SLAB), :] for j in range(SLAB)], axis=1)
        h = (_rms(x2) * gmoe_ref[...]).astype(BF16)
        lt = lax.dot_general(wrh_ref[...], h, (((1,), (1,)), ((), ())),
                             preferred_element_type=F32) + br_ref[...]
        rows = lax.broadcasted_iota(jnp.int32, lt.shape, 0)
        e_lo, e_hi = lo_ref[i], hi_ref[i]
        pick = lambda r: jnp.sum(jnp.where(rows == r, lt, 0.0), axis=0, keepdims=True)
        is_grp = rows < N_GROUPS
        gmax = jnp.max(jnp.where(is_grp, lt, -jnp.inf), axis=0, keepdims=True)
        p_grp = (jnp.exp(pick(e_lo // EXPERTS_PER_GROUP) - gmax)
                 / jnp.sum(jnp.where(is_grp, jnp.exp(lt - gmax), 0.0), axis=0, keepdims=True))
        s_lo, s_hi = pick(N_GROUPS + e_lo), pick(N_GROUPS + e_hi)
        smax = jnp.maximum(s_lo, s_hi)
        x_lo, x_hi = jnp.exp(s_lo - smax), jnp.exp(s_hi - smax)
        w_lo = p_grp * x_lo / (x_lo + x_hi)
        w_hi = p_grp * x_hi / (x_lo + x_hi)
        rw = lax.broadcasted_iota(jnp.int32, (LANE, ts), 0)
        gates = jnp.where(rw == 0, w_lo, jnp.where(rw == 1, w_hi, 0.0)).T
        y = x2
        for col, gu_ref, dn_ref in ((0, gu_lo_ref, dn_lo_ref), (1, gu_hi_ref, dn_hi_ref)):
            gu = jnp.dot(h, gu_ref[0], preferred_element_type=F32)
            act = jax.nn.silu(gu[:, :D_EXPERT]) * gu[:, D_EXPERT:]
            y = y + gates[:, col:col + 1] * jnp.dot(act.astype(BF16), dn_ref[0], preferred_element_type=F32)
        for j in range(SLAB):
            o_ref[pl.ds(j, ts, stride=SLAB), :] = y[:, j * LANE:(j + 1) * LANE]

    @pl.when(valid_ref[i] == 0)
    def _():
        o_ref[...] = jnp.zeros_like(o_ref)


def _moe(xs, tile_lo, tile_hi, tile_valid, gmoe, wr_hi, br, w_gu_bf, w_dn_bf, ts):
    n_tiles = tile_lo.shape[0]
    D = D_MODEL
    const = lambda shape: pl.BlockSpec(shape, lambda i, lo, hi, va: (0,) * len(shape))
    return pl.pallas_call(
        _moe_kernel,
        grid_spec=pltpu.PrefetchScalarGridSpec(
            num_scalar_prefetch=3,
            grid=(n_tiles,),
            in_specs=[
                pl.BlockSpec((ts * SLAB, LANE), lambda i, lo, hi, va: (i, 0)),
                const((1, D)), const((ROUTE_ROWS, D)), const((ROUTE_ROWS, 1)),
                pl.BlockSpec((1, D, 2 * D_EXPERT), lambda i, lo, hi, va: (lo[i], 0, 0)),
                pl.BlockSpec((1, D_EXPERT, D), lambda i, lo, hi, va: (lo[i], 0, 0)),
                pl.BlockSpec((1, D, 2 * D_EXPERT), lambda i, lo, hi, va: (hi[i], 0, 0)),
                pl.BlockSpec((1, D_EXPERT, D), lambda i, lo, hi, va: (hi[i], 0, 0)),
            ],
            out_specs=pl.BlockSpec((ts * SLAB, LANE), lambda i, lo, hi, va: (i, 0)),
        ),
        out_shape=jax.ShapeDtypeStruct((n_tiles * ts * SLAB, LANE), F32),
        compiler_params=pltpu.CompilerParams(dimension_semantics=("arbitrary",),
                                             vmem_limit_bytes=VMEM_LIMIT),
        name="moe_experts",
    )(tile_lo, tile_hi, tile_valid, xs, gmoe, wr_hi, br, w_gu_bf, w_dn_bf, w_gu_bf, w_dn_bf)


def _route_tables(route, counts, tm, ts):
    T = route.shape[1]
    cls = route[0].astype(jnp.int32)
    rank = route[1].astype(jnp.int32)
    cnt = counts[:, :N_CLASSES, LANE - 1].astype(jnp.int32)
    total = jnp.sum(cnt, axis=0)
    tiles_c = (total + ts - 1) // ts
    tile_start = jnp.cumsum(tiles_c) - tiles_c
    before = jnp.cumsum(cnt, axis=0) - cnt
    base = tile_start[None, :] * ts + before
    classes = jnp.arange(N_CLASSES, dtype=jnp.int32)
    pick = cls.reshape(-1, tm)[:, :, None] == classes[None, None, :]
    dest = jnp.sum(jnp.where(pick, base[:, None, :], 0), axis=-1).reshape(T) + rank
    n_tiles = T // ts + N_CLASSES
    tile_id = jnp.arange(n_tiles, dtype=jnp.int32)
    used = jnp.sum(tiles_c)
    ends = jnp.cumsum(tiles_c)
    tcls = jnp.sum(jnp.minimum(tile_id, used - 1)[:, None] >= ends[None, :], axis=1).astype(jnp.int32)
    tcls = jnp.minimum(tcls, N_CLASSES - 1)
    grp, pr = tcls // len(PAIRS), tcls % len(PAIRS)
    pair_lo = (pr >= 3).astype(jnp.int32) + (pr >= 5).astype(jnp.int32)
    pair_hi = jnp.where(pr < 3, pr + 1, jnp.where(pr < 5, pr - 1, 3))
    tile_lo = grp * EXPERTS_PER_GROUP + pair_lo
    tile_hi = grp * EXPERTS_PER_GROUP + pair_hi
    return dest, tile_lo, tile_hi, (tile_id < used).astype(jnp.int32)


def _moe_block(x2s, route, counts, gmoe, wr_hi, br, w_gu_bf, w_dn_bf, tm):
    T = route.shape[1]
    ts = _pick_tile(T, 256)
    dest, tile_lo, tile_hi, tile_valid = _route_tables(route, counts, tm, ts)
    n_tiles = T // ts + N_CLASSES
    xs = _scatter_rows(x2s, dest, n_tiles * ts, _pick_tile(T, 1024))
    ys = _moe(xs, tile_lo, tile_hi, tile_valid, gmoe, wr_hi, br, w_gu_bf, w_dn_bf, ts)
    return _gather_rows(ys, dest, tm)


def _pick_tile(n, pref):
    t = min(pref, n)
    while n % t:
        t //= 2
    return t


def kernel(x, mem, norm_mix, w_in, hyena_short_w, hyena_short_b, filt_w1, filt_b1, filt_w2, filt_b2, filt_w3, filt_sin_freq, hyena_skip, attn_q_norm, attn_k_norm, out_norm_hyena, out_norm_attn, w_out, norm_xattn, norm_mem, w_q_mem, w_k_mem, w_v_mem, mem_q_norm, mem_k_norm, w_o_mem, norm_moe, w_router_grp, b_router_grp, w_router_exp, b_router_exp, w_gate, w_up, w_down):
    B, L, D = x.shape
    T = B * L
    NR = L // GRID_W
    depth = norm_mix.shape[0]
    tm = _pick_tile(L, 1024)
    tq = _pick_tile(L, 512)
    tk = _pick_tile(L, 512)
    cos_t, sin_t = _rope_tables(L)

    def to_column_major(a):
        return a.reshape(B, NR, GRID_W, -1).transpose(0, 2, 1, 3).reshape(T, -1)

    def to_row_major(a):
        return a.reshape(B, GRID_W, NR, -1).transpose(0, 2, 1, 3).reshape(T, -1)

    for i in range(depth):
        qkg = jnp.concatenate([jnp.tile(attn_q_norm[i], N_Q_HEADS), jnp.tile(attn_k_norm[i], N_KV_HEADS)])[None, :]
        u, q, k, v = _inproj(x, norm_mix[i][None, :], w_in[i].astype(BF16), cos_t, sin_t, qkg, tm)
        y_hy = to_row_major(_hyena(to_column_major(u), hyena_short_w[i], hyena_short_b[i], hyena_skip[i],
                                   filt_w1[i], filt_b1[i], filt_w2[i], filt_b2[i], filt_w3[i],
                                   filt_sin_freq[i], B, L))
        y_at = _attention(q, k, v, out_norm_attn[i][None, :], tq, tk)
        km, vm = _memkv(mem, norm_mem[i][None, :], w_k_mem[i].astype(BF16), w_v_mem[i].astype(BF16),
                        mem_k_norm[i][None, :])
        wr = jnp.concatenate([w_router_grp[i], w_router_exp[i]], axis=1).T
        wr = jnp.pad(wr, ((0, ROUTE_ROWS - wr.shape[0]), (0, 0)))
        wr_hi, wr_lo = _split_bf16(wr)
        br = jnp.pad(jnp.concatenate([b_router_grp[i], b_router_exp[i]]), (0, ROUTE_ROWS - N_GROUPS - N_EXPERTS))
        gmoe = norm_moe[i][None, :]
        x2s, route, counts = _mix(x, y_hy, y_at, out_norm_hyena[i][None, :], w_out[i].astype(BF16),
                                  norm_xattn[i][None, :], w_q_mem[i].astype(BF16), mem_q_norm[i][None, :],
                                  km, vm, w_o_mem[i].astype(BF16), gmoe, wr_hi, wr_lo, br[:, None], tm)
        w_gu = jnp.concatenate([w_gate[i], w_up[i]], axis=-1).astype(BF16)
        x = _moe_block(x2s, route, counts, gmoe, wr_hi, br[:, None], w_gu, w_down[i].astype(BF16),
                       tm).reshape(B, L, D)
    return x
```

```python
import functools
import math

import jax
import jax.numpy as jnp
import numpy as np
from jax import lax
from jax.experimental import pallas as pl
from jax.experimental.pallas import tpu as pltpu

F32 = jnp.float32
BF16 = jnp.bfloat16

D_MODEL = 1024
GRID_W = 64
D_HYENA = 512
HYENA_GROUPS = 8
HYENA_ORDER = 2
SHORT_CONV = 3
FILTER_BANDS = 16
FILTER_FF = 64
DECAY_TARGET = 1e-2
FAST_DECAY_PCT = 0.3
SLOW_DECAY_PCT = 1.5
N_Q_HEADS = 8
N_KV_HEADS = 2
HEAD_DIM = 64
D_ATTN = N_Q_HEADS * HEAD_DIM
D_KV = N_KV_HEADS * HEAD_DIM
ROPE_THETA = 10000.0
D_U = (HYENA_ORDER + 1) * D_HYENA
D_IN = D_U + D_ATTN + 2 * D_KV
D_QK = D_ATTN + D_KV
MEM_HEADS = 4
MEM_HEAD_DIM = 128
D_MEM_ATTN = MEM_HEADS * MEM_HEAD_DIM
N_GROUPS = 4
EXPERTS_PER_GROUP = 4
N_EXPERTS = N_GROUPS * EXPERTS_PER_GROUP
D_EXPERT = 512
EPS = 1e-6

PAIRS = ((0, 1), (0, 2), (0, 3), (1, 2), (1, 3), (2, 3))
N_CLASSES = N_GROUPS * len(PAIRS)
ROUTE_ROWS = 32
LANE = 128
SUBLANE = 8
SLAB = SUBLANE
VMEM_LIMIT = 56 * 1024 * 1024
HY_CB = 256
HY_PAD = 8
HY_UNROLL = 4
DMA_UNROLL = 8
LOG2E = math.log2(math.e)


def _rms(x):
    return x * lax.rsqrt(jnp.mean(x * x, axis=-1, keepdims=True) + EPS)


def _group_sumsq_matrix(width, group):
    idx = np.arange(width) // group
    return jnp.asarray((idx[:, None] == idx[None, :]).astype(np.float32) / group, dtype=BF16)


def _hy_pitch(n1):
    return 4 * n1 + HY_PAD


def _inproj_kernel(x_ref, g_ref, w_ref, cos_ref, sin_ref, qkg_ref, gmat_ref,
                   u_ref, q_ref, k_ref, v_ref):
    x = x_ref[...]
    tm = x.shape[0]
    h = _rms(x) * g_ref[...]
    p = jnp.dot(h.astype(BF16), w_ref[...], preferred_element_type=F32)
    u_ref[...] = p[:, :D_U].astype(BF16)
    qk = p[:, D_U:D_U + D_QK]
    ms = jnp.dot((qk * qk).astype(BF16), gmat_ref[...], preferred_element_type=F32)
    qk = qk * lax.rsqrt(ms + EPS) * qkg_ref[...]
    reps = D_QK // LANE
    cos = jnp.concatenate([cos_ref[...]] * reps, axis=1)
    sin = jnp.concatenate([sin_ref[...]] * reps, axis=1)
    lane = lax.broadcasted_iota(jnp.int32, (tm, D_QK), 1)
    partner = jnp.where((lane % 32) < 16, pltpu.roll(qk, D_QK - 16, 1), pltpu.roll(qk, 16, 1))
    qk = qk * cos + partner * sin
    for hd in range(N_Q_HEADS):
        q_ref[0, hd] = (qk[:, hd * HEAD_DIM:(hd + 1) * HEAD_DIM] * (HEAD_DIM ** -0.5 * LOG2E)).astype(BF16)
    for hd in range(N_KV_HEADS):
        k_ref[0, hd] = qk[:, D_ATTN + hd * HEAD_DIM:D_ATTN + (hd + 1) * HEAD_DIM].astype(BF16)
    vv = p[:, D_U + D_QK:]
    lane_v = lax.broadcasted_iota(jnp.int32, (tm, LANE), 1)
    ones_col = jnp.where(lane_v == HEAD_DIM, 1.0, 0.0)
    v_ref[0, 0] = jnp.where(lane_v < HEAD_DIM, vv, ones_col).astype(BF16)
    v_ref[0, 1] = jnp.where(lane_v < HEAD_DIM, pltpu.roll(vv, HEAD_DIM, 1), ones_col).astype(BF16)


def _inproj(x, g, w_bf, cos_t, sin_t, qkg, tm):
    B, L, D = x.shape
    nt = L // tm
    gmat = _group_sumsq_matrix(D_QK, HEAD_DIM)
    return pl.pallas_call(
        _inproj_kernel,
        grid=(B * nt,),
        in_specs=[
            pl.BlockSpec((tm, D), lambda i: (i, 0)),
            pl.BlockSpec((1, D), lambda i: (0, 0)),
            pl.BlockSpec((D, D_IN), lambda i: (0, 0)),
            pl.BlockSpec((tm, LANE), lambda i: (i % nt, 0)),
            pl.BlockSpec((tm, LANE), lambda i: (i % nt, 0)),
            pl.BlockSpec((1, D_QK), lambda i: (0, 0)),
            pl.BlockSpec((D_QK, D_QK), lambda i: (0, 0)),
        ],
        out_specs=[
            pl.BlockSpec((tm, D_U), lambda i: (i, 0)),
            pl.BlockSpec((1, N_Q_HEADS, tm, HEAD_DIM), lambda i: (i // nt, 0, i % nt, 0)),
            pl.BlockSpec((1, N_KV_HEADS, tm, HEAD_DIM), lambda i: (i // nt, 0, i % nt, 0)),
            pl.BlockSpec((1, N_KV_HEADS, tm, LANE), lambda i: (i // nt, 0, i % nt, 0)),
        ],
        out_shape=[
            jax.ShapeDtypeStruct((B * L, D_U), BF16),
            jax.ShapeDtypeStruct((B, N_Q_HEADS, L, HEAD_DIM), BF16),
            jax.ShapeDtypeStruct((B, N_KV_HEADS, L, HEAD_DIM), BF16),
            jax.ShapeDtypeStruct((B, N_KV_HEADS, L, LANE), BF16),
        ],
        compiler_params=pltpu.CompilerParams(dimension_semantics=("parallel",),
                                             vmem_limit_bytes=VMEM_LIMIT),
        name="inproj",
    )(x.reshape(B * L, D), g, w_bf, cos_t, sin_t, qkg, gmat)


def _rope_tables(L):
    p = np.arange(L)
    row, col = p // GRID_W, p % GRID_W
    d = np.arange(LANE) % HEAD_DIM
    sec, r = d // 32, d % 32
    inv = (ROPE_THETA ** (-(r % 16).astype(np.float64) / 16.0)).astype(np.float32)
    pos = np.where(sec[None, :] == 0, row[:, None], col[:, None]).astype(np.float32)
    ang = pos * inv[None, :]
    sign = np.where(r < 16, -1.0, 1.0)[None, :]
    return jnp.asarray(np.cos(ang), F32), jnp.asarray(np.sin(ang) * sign, F32)


def _real_embed(m):
    return np.block([[m.real, -m.imag], [m.imag, m.real]])


@functools.lru_cache(maxsize=None)
def _dft_tables(n1, n2):
    L = n1 * n2
    i1, i2 = np.arange(n1), np.arange(n2)
    om = np.exp(-1j * np.pi / L)
    f1 = np.exp(-2j * np.pi * np.outer(i1, i1) / n1)
    tw = np.exp(-2j * np.pi * np.outer(i1, i2) / L)
    fwd1, fwd1_real = [], []
    for t in i2:
        ev = tw[:, t][:, None] * f1
        od = (om ** t) * tw[:, t][:, None] * f1 * (om ** (n2 * i1))[None, :]
        fwd1.append(np.concatenate([_real_embed(ev), _real_embed(od)], axis=0))
        z = np.zeros((n1, n1))
        fwd1_real.append(np.block([[ev.real, z], [ev.imag, z], [z, od.real], [z, od.imag]]))
    f2 = _real_embed(np.exp(-2j * np.pi * np.outer(i2, i2) / n2))
    f2i = np.exp(2j * np.pi * np.outer(i2, i2) / n2)
    inv_e = np.stack([_real_embed(np.conj(tw[k, :])[:, None] * f2i) for k in i1])
    inv_o = np.stack([_real_embed((om ** (-i2))[:, None] * np.conj(tw[k, :])[:, None] * f2i) for k in i1])
    f1i = np.exp(2j * np.pi * np.outer(i1, i1) / n1)
    inv2 = np.concatenate([_real_embed(f1i), _real_embed((om ** (-n2 * i1))[:, None] * f1i)], axis=1) / (2 * L)
    as_f32 = lambda a: np.asarray(a, np.float32)
    return {"fwd1": as_f32(np.stack(fwd1)), "fwd1_real": as_f32(np.stack(fwd1_real)), "fwd2": as_f32(f2),
            "inv_e": as_f32(inv_e), "inv_o": as_f32(inv_o), "inv2": as_f32(inv2)}


def _aligned(start, align):
    return start if isinstance(start, int) else pl.multiple_of(start, align)


def _conv_block(ref, w_ref, b_ref, base, t2, n1, n2, edge):
    def blk(i):
        return ref[pl.ds(_aligned(base + i * n1, n1), n1), :].astype(F32)

    cur = blk(t2)
    if not edge:
        prev, nxt = blk(jnp.maximum(t2 - 1, 0)), blk(jnp.minimum(t2 + 1, n2 - 1))
    else:
        row = lax.broadcasted_iota(jnp.int32, cur.shape, 0)
        if t2 == 0:
            prev = jnp.where(row == 0, 0.0, pltpu.roll(blk(n2 - 1), 1, 0))
            nxt = blk(1)
        else:
            prev = blk(n2 - 2)
            nxt = jnp.where(row == n1 - 1, 0.0, pltpu.roll(blk(0), n1 - 1, 0))
    return prev * w_ref[0:1, :] + cur * w_ref[1:2, :] + nxt * w_ref[2:3, :] + b_ref[...]


def _plain_block(ref, base, t2, n1):
    return ref[pl.ds(_aligned(base + t2 * n1, n1), n1), :].astype(F32)


def _store_slabs(a_ref, rows, val):
    for s in range(val.shape[1] // LANE):
        a_ref[s, rows, :] = val[:, s * LANE:(s + 1) * LANE]


def _load_slabs(a_ref, rows):
    return jnp.concatenate([a_ref[s, rows, :] for s in range(a_ref.shape[0])], axis=1)


def _forward_stage1(load_stacked, fwd1_ref, a_ref, n1, n2, redo_edges):
    pitch = _hy_pitch(n1)

    def body(t2, edge):
        a = jnp.dot(fwd1_ref[t2], load_stacked(t2, edge).astype(BF16), preferred_element_type=F32)
        _store_slabs(a_ref, pl.ds(_aligned(t2 * pitch, SUBLANE), 4 * n1), a)

    def loop_body(t2, carry):
        body(t2, False)
        return carry

    lax.fori_loop(0, n2, loop_body, 0, unroll=HY_UNROLL)
    if redo_edges:
        body(0, True)
        body(n2 - 1, True)


def _forward_stage2(a_ref, fwd2_ref, k1s, n1, n2):
    pitch = _hy_pitch(n1)
    cols = []
    for k1 in k1s:
        for half in range(2):
            cols.append(jnp.concatenate(
                [_load_slabs(a_ref, pl.ds((2 * half + part) * n1 + k1, n2, stride=pitch)) for part in range(2)],
                axis=0))
    x = jnp.dot(fwd2_ref[...], jnp.concatenate(cols, axis=1).astype(BF16), preferred_element_type=F32)
    w2 = x.shape[1] // len(k1s)
    return [(x[:n2, i * w2:(i + 1) * w2], x[n2:, i * w2:(i + 1) * w2]) for i in range(len(k1s))]


def _hyena_kernel(*refs, n1, n2, conv_z):
    L = n1 * n2
    if conv_z:
        (z_ref, zw_ref, zb_ref, g_ref, gw_ref, gb_ref, ker_ref, kei_ref, kor_ref, koi_ref,
         skip_ref, fwd1_ref, fwd2_ref, inve_ref, invo_ref, inv2_ref, o_ref, a_ref) = refs
        load_z = lambda part, t2, edge: _conv_block(z_ref, zw_ref, zb_ref, part * L, t2, n1, n2, edge)
    else:
        (z_ref, g_ref, gw_ref, gb_ref, ker_ref, kei_ref, kor_ref, koi_ref,
         skip_ref, fwd1_ref, fwd2_ref, inve_ref, invo_ref, inv2_ref, o_ref, a_ref) = refs
        load_z = lambda part, t2, edge: _plain_block(z_ref, part * L, t2, n1)
    cb = z_ref.shape[1]
    pitch = _hy_pitch(n1)

    _forward_stage1(lambda t2, edge: jnp.concatenate([load_z(0, t2, edge), load_z(1, t2, edge)], axis=0),
                    fwd1_ref, a_ref, n1, n2, redo_edges=conv_z)

    skip2 = jnp.concatenate([skip_ref[...], skip_ref[...]], axis=1)
    group = HY_UNROLL if n1 % HY_UNROLL == 0 else 1

    def freq_body(i, carry):
        k1s = [i * group + j for j in range(group)]
        results = []
        for k1, (xr, xi) in zip(k1s, _forward_stage2(a_ref, fwd2_ref, k1s, n1, n2)):
            rows = pl.ds(pl.multiple_of(k1 * n2, n2), n2)
            kr = jnp.concatenate([ker_ref[rows, :], kor_ref[rows, :]], axis=1).astype(F32) + skip2
            ki = jnp.concatenate([kei_ref[rows, :], koi_ref[rows, :]], axis=1).astype(F32)
            yr = xr * kr - xi * ki
            yi = xr * ki + xi * kr
            ye = jnp.concatenate([yr[:, :cb], yi[:, :cb]], axis=0).astype(BF16)
            yo = jnp.concatenate([yr[:, cb:], yi[:, cb:]], axis=0).astype(BF16)
            be = jnp.dot(inve_ref[k1], ye, preferred_element_type=F32)
            bo = jnp.dot(invo_ref[k1], yo, preferred_element_type=F32)
            results.append((be[:n2], be[n2:], bo[:n2], bo[n2:]))
        for k1, quarters in zip(k1s, results):
            for q, val in enumerate(quarters):
                _store_slabs(a_ref, pl.ds(q * n1 + k1, n2, stride=pitch), val)
        return carry

    lax.fori_loop(0, n1 // group, freq_body, 0)

    def time_block(t2, edge):
        blk = _load_slabs(a_ref, pl.ds(_aligned(t2 * pitch, SUBLANE), 4 * n1))
        y = jnp.dot(inv2_ref[...], blk.astype(BF16), preferred_element_type=F32)
        for part in range(2):
            gate = _conv_block(g_ref, gw_ref, gb_ref, part * L, t2, n1, n2, edge)
            o_ref[pl.ds(_aligned(part * L + t2 * n1, n1), n1), :] = (
                gate * y[part * n1:(part + 1) * n1]).astype(o_ref.dtype)

    def time_body(t2, carry):
        time_block(t2, False)
        return carry

    lax.fori_loop(0, n2, time_body, 0, unroll=HY_UNROLL)
    time_block(0, True)
    time_block(n2 - 1, True)


def _hyena_order(z, z_col0, u, gate_col0, short_w, short_b, spectra, spec_col0, skip, tabs, B, L, conv_z):
    n2 = GRID_W
    n1 = L // n2
    cb = HY_CB
    ncb = D_HYENA // cb
    pitch = _hy_pitch(n1)
    col = lambda off: (lambda c, p: (0, off + c))
    pair = lambda off: (lambda c, p: (p, off + c))
    one = pl.Buffered(1)
    const3 = lambda a: pl.BlockSpec(a.shape, lambda c, p: (0, 0, 0))
    const2 = lambda a: pl.BlockSpec(a.shape, lambda c, p: (0, 0))
    in_specs = [pl.BlockSpec((2 * L, cb), pair(z_col0), pipeline_mode=one)]
    args = [z]
    if conv_z:
        in_specs += [pl.BlockSpec((SHORT_CONV, cb), col(z_col0)), pl.BlockSpec((1, cb), col(z_col0))]
        args += [short_w, short_b]
    in_specs += [pl.BlockSpec((2 * L, cb), pair(gate_col0), pipeline_mode=one),
                 pl.BlockSpec((SHORT_CONV, cb), col(gate_col0)), pl.BlockSpec((1, cb), col(gate_col0))]
    args += [u, short_w, short_b]
    in_specs += [pl.BlockSpec((L, cb), col(spec_col0), pipeline_mode=one)] * 4
    args += list(spectra)
    in_specs += [pl.BlockSpec((1, cb), lambda c, p: (0, c)),
                 const3(tabs["fwd1"]), const2(tabs["fwd2"]), const3(tabs["inv_e"]), const3(tabs["inv_o"]),
                 const2(tabs["inv2"])]
    args += [skip, tabs["fwd1"], tabs["fwd2"], tabs["inv_e"], tabs["inv_o"], tabs["inv2"]]
    return pl.pallas_call(
        functools.partial(_hyena_kernel, n1=n1, n2=n2, conv_z=conv_z),
        grid=(ncb, B // 2),
        in_specs=in_specs,
        out_specs=pl.BlockSpec((2 * L, cb), lambda c, p: (p, c)),
        out_shape=jax.ShapeDtypeStruct((B * L, D_HYENA), BF16),
        scratch_shapes=[pltpu.VMEM((cb // LANE, n2 * pitch, LANE), F32)],
        compiler_params=pltpu.CompilerParams(dimension_semantics=("arbitrary", "arbitrary"),
                                             vmem_limit_bytes=VMEM_LIMIT),
        name="hyena_order",
    )(*args)


def _spectra_kernel(kc_ref, kn_ref, fwd1_ref, fwd2_ref, er_ref, ei_ref, or_ref, oi_ref, a_ref, *, n1, n2):
    cb = kc_ref.shape[1]
    _forward_stage1(lambda t2, edge: jnp.concatenate([_plain_block(kc_ref, 0, t2, n1),
                                                     _plain_block(kn_ref, 0, t2, n1)], axis=0),
                    fwd1_ref, a_ref, n1, n2, redo_edges=False)

    def freq_body(k1, carry):
        (xr, xi), = _forward_stage2(a_ref, fwd2_ref, [k1], n1, n2)
        rows = pl.ds(pl.multiple_of(k1 * n2, n2), n2)
        er_ref[rows, :] = xr[:, :cb].astype(BF16)
        ei_ref[rows, :] = xi[:, :cb].astype(BF16)
        or_ref[rows, :] = xr[:, cb:].astype(BF16)
        oi_ref[rows, :] = xi[:, cb:].astype(BF16)
        return carry

    lax.fori_loop(0, n1, freq_body, 0, unroll=HY_UNROLL)


def _filter_spectra(kc, kn, tabs, L):
    n2 = GRID_W
    n1 = L // n2
    cb = HY_CB
    C = kc.shape[1]
    pitch = _hy_pitch(n1)
    return pl.pallas_call(
        functools.partial(_spectra_kernel, n1=n1, n2=n2),
        grid=(C // cb,),
        in_specs=[pl.BlockSpec((L, cb), lambda c: (0, c)), pl.BlockSpec((L, cb), lambda c: (0, c)),
                  pl.BlockSpec(tabs["fwd1_real"].shape, lambda c: (0, 0, 0)),
                  pl.BlockSpec(tabs["fwd2"].shape, lambda c: (0, 0))],
        out_specs=[pl.BlockSpec((L, cb), lambda c: (0, c))] * 4,
        out_shape=[jax.ShapeDtypeStruct((L, C), BF16)] * 4,
        scratch_shapes=[pltpu.VMEM((cb // LANE, n2 * pitch, LANE), F32)],
        compiler_params=pltpu.CompilerParams(dimension_semantics=("arbitrary",),
                                             vmem_limit_bytes=VMEM_LIMIT),
        name="filter_spectra",
    )(kc, kn, tabs["fwd1_real"], tabs["fwd2"])


def _filter_time_kernel(zf_ref, zr_ref, w1_ref, b1_ref, w2_ref, b2_ref, w3f_ref, w3b_ref, sf_ref,
                        dl_ref, kc_ref, kn_ref, hf_ref, hr_ref):
    hp = lax.Precision.HIGHEST

    def dot3(a, b):
        a_hi, a_lo = _split_bf16(a)
        b_hi, b_lo = _split_bf16(b)
        return (jnp.dot(a_hi, b_hi, preferred_element_type=F32) + jnp.dot(a_hi, b_lo, preferred_element_type=F32)
                + jnp.dot(a_lo, b_hi, preferred_element_type=F32))

    df = jnp.exp(-zf_ref[:, 0:1] * dl_ref[...])
    row = lax.broadcasted_iota(jnp.int32, df.shape, 0)
    dr = jnp.where(row > 0, jnp.exp(-zr_ref[:, 0:1] * dl_ref[...]), 0.0)

    @pl.when(pl.program_id(0) == 0)
    def _():
        for z_ref, h_ref in ((zf_ref, hf_ref), (zr_ref, hr_ref)):
            a = jnp.sin(sf_ref[0:1, :] * (jnp.dot(z_ref[...], w1_ref[...], precision=hp,
                                                  preferred_element_type=F32) + b1_ref[...]))
            h_ref[...] = jnp.sin(sf_ref[1:2, :] * (jnp.dot(a, w2_ref[...], precision=hp,
                                                           preferred_element_type=F32) + b2_ref[...]))

    fwd = dot3(hf_ref[...], w3f_ref[...]) * df
    bwd = dot3(hr_ref[...], w3b_ref[...]) * dr
    norm = (jnp.sum(jnp.abs(fwd), axis=0, keepdims=True) + jnp.sum(jnp.abs(bwd), axis=0, keepdims=True)
            + EPS)
    kc_ref[...] = (fwd + bwd) / norm
    kn_ref[...] = (fwd - bwd) / norm


def _filter_time(w1, b1, w2, b2, w3, sin_freq, L):
    NR = L // GRID_W
    f64 = np.float64
    p = np.arange(L)
    t_nat = (p % NR) * GRID_W + p // NR
    tl = np.linspace(0.0, 1.0, L)
    bands = np.linspace(1e-4, FILTER_BANDS - 1, FILTER_BANDS)
    wv = (2.0 * math.pi / L) * np.arange(L, dtype=f64)
    feats = np.concatenate([tl[:, None], np.cos(bands[None, :] * wv[:, None]),
                            np.sin(bands[None, :] * wv[:, None])], axis=1)
    deltas = np.abs(np.linspace(math.log(DECAY_TARGET) / SLOW_DECAY_PCT,
                                math.log(DECAY_TARGET) / FAST_DECAY_PCT, D_HYENA))
    t_rev = (L - t_nat) % L
    pad = LANE - feats.shape[1]
    zf = np.pad(feats[t_nat], ((0, 0), (0, pad)))
    zr = np.pad(feats[t_rev], ((0, 0), (0, pad)))
    dl = np.tile(deltas, HYENA_ORDER)[None, :]
    w1p = jnp.pad(w1, ((0, pad), (0, 0)))
    w3r = w3.reshape(FILTER_FF, HYENA_ORDER, 2, D_HYENA)
    w3f = w3r[:, :, 0].reshape(FILTER_FF, HYENA_ORDER * D_HYENA)
    w3b = w3r[:, :, 1].reshape(FILTER_FF, HYENA_ORDER * D_HYENA)
    C = HYENA_ORDER * D_HYENA
    cb = HY_CB
    full = lambda a: pl.BlockSpec(a.shape, lambda c: (0,) * a.ndim)
    colb = lambda r: pl.BlockSpec((r, cb), lambda c: (0, c))
    zf, zr, dl = (jnp.asarray(a, F32) for a in (zf, zr, dl))
    args = (zf, zr, w1p, b1[None, :], w2, b2[None, :], w3f, w3b, sin_freq, dl)
    return pl.pallas_call(
        _filter_time_kernel,
        grid=(C // cb,),
        in_specs=[full(zf), full(zr), full(w1p), full(args[3]), full(w2), full(args[5]),
                  colb(FILTER_FF), colb(FILTER_FF), full(sin_freq), colb(1)],
        out_specs=[colb(L), colb(L)],
        out_shape=[jax.ShapeDtypeStruct((L, C), F32)] * 2,
        scratch_shapes=[pltpu.VMEM((L, FILTER_FF), F32)] * 2,
        compiler_params=pltpu.CompilerParams(dimension_semantics=("arbitrary",),
                                             vmem_limit_bytes=VMEM_LIMIT),
        name="filter_time",
    )(*args)


def _hyena(u, short_w, short_b, skip, w1, b1, w2, b2, w3, sin_freq, B, L):
    tabs = {name: jnp.asarray(t).astype(BF16) for name, t in _dft_tables(L // GRID_W, GRID_W).items()}
    kc, kn = _filter_time(w1, b1, w2, b2, w3, sin_freq, L)
    spectra = _filter_spectra(kc, kn, tabs, L)
    ncb = D_HYENA // HY_CB
    sb = short_b[None, :]
    z1 = _hyena_order(u, 0, u, ncb, short_w, sb, spectra, 0, skip[0:1], tabs, B, L, True)
    return _hyena_order(z1, 0, u, 2 * ncb, short_w, sb, spectra, ncb, skip[1:2], tabs, B, L, False)


def _attn_kernel(q_ref, k_ref, v_ref, g_ref, st_ref, o_ref, *, tk):
    G, tq = q_ref.shape[1], q_ref.shape[2]
    L = k_ref.shape[2]
    R = G * tq
    q = q_ref[0].reshape(R, HEAD_DIM)
    m = jnp.full((R, 1), -jnp.inf, F32)
    acc = jnp.zeros((R, LANE), F32)
    for c in range(L // tk):
        kc = k_ref[0, 0, c * tk:(c + 1) * tk, :]
        s = lax.dot_general(q, kc, (((1,), (1,)), ((), ())), preferred_element_type=F32)
        m_new = jnp.maximum(m, jnp.max(s, axis=-1, keepdims=True))
        p = jnp.exp2((s - m_new).astype(BF16))
        acc = jnp.exp2(m - m_new) * acc + jnp.dot(p, v_ref[0, 0, c * tk:(c + 1) * tk, :],
                                                   preferred_element_type=F32)
        m = m_new
    a_hi, a_lo = _split_bf16(acc * acc)
    ms = (jnp.dot(a_hi, st_ref[:, :LANE], preferred_element_type=F32)
          + jnp.dot(a_lo, st_ref[:, :LANE], preferred_element_type=F32))
    l_hi, l_lo = _split_bf16(acc)
    l = (jnp.dot(l_hi, st_ref[:, LANE:], preferred_element_type=F32)
         + jnp.dot(l_lo, st_ref[:, LANE:], preferred_element_type=F32))
    o = (acc * lax.rsqrt(ms + EPS * l * l))[:, :HEAD_DIM]
    o = jnp.concatenate([o[h * tq:(h + 1) * tq] for h in range(G)], axis=1)
    o_ref[0] = (o * g_ref[...]).astype(o_ref.dtype)


def _attention(q, k, v, g_out, tq, tk):
    B, H, L, _ = q.shape
    G = H // N_KV_HEADS
    st = np.zeros((LANE, 2 * LANE), np.float32)
    st[:HEAD_DIM, :LANE] = 1.0 / HEAD_DIM
    st[HEAD_DIM, LANE:] = 1.0
    st = jnp.asarray(st, dtype=BF16)
    return pl.pallas_call(
        functools.partial(_attn_kernel, tk=tk),
        grid=(B, N_KV_HEADS, L // tq),
        in_specs=[
            pl.BlockSpec((1, G, tq, HEAD_DIM), lambda b, kv, i: (b, kv, i, 0)),
            pl.BlockSpec((1, 1, L, HEAD_DIM), lambda b, kv, i: (b, kv, 0, 0)),
            pl.BlockSpec((1, 1, L, LANE), lambda b, kv, i: (b, kv, 0, 0)),
            pl.BlockSpec((1, G * HEAD_DIM), lambda b, kv, i: (0, kv)),
            pl.BlockSpec((LANE, 2 * LANE), lambda b, kv, i: (0, 0)),
        ],
        out_specs=pl.BlockSpec((1, tq, G * HEAD_DIM), lambda b, kv, i: (b, i, kv)),
        out_shape=jax.ShapeDtypeStruct((B, L, D_ATTN), BF16),
        compiler_params=pltpu.CompilerParams(dimension_semantics=("parallel", "parallel", "parallel"),
                                             vmem_limit_bytes=VMEM_LIMIT),
        name="attention",
    )(q, k, v, g_out, st)


def _memkv_kernel(m_ref, g_ref, wk_ref, wv_ref, kg_ref, k_ref, v_ref):
    h = (_rms(m_ref[0]) * g_ref[...]).astype(BF16)
    k = jnp.dot(h, wk_ref[...], preferred_element_type=F32)
    v = jnp.dot(h, wv_ref[...], preferred_element_type=F32)
    kn = [_rms(k[:, i * MEM_HEAD_DIM:(i + 1) * MEM_HEAD_DIM]) * kg_ref[...] for i in range(MEM_HEADS)]
    k_ref[0] = jnp.concatenate(kn, axis=1).astype(BF16)
    v_ref[0] = v.astype(BF16)


def _memkv(mem, g, wk_bf, wv_bf, kg):
    B, M, D = mem.shape
    return pl.pallas_call(
        _memkv_kernel,
        grid=(B,),
        in_specs=[
            pl.BlockSpec((1, M, D), lambda b: (b, 0, 0)),
            pl.BlockSpec((1, D), lambda b: (0, 0)),
            pl.BlockSpec((D, D_MEM_ATTN), lambda b: (0, 0)),
            pl.BlockSpec((D, D_MEM_ATTN), lambda b: (0, 0)),
            pl.BlockSpec((1, MEM_HEAD_DIM), lambda b: (0, 0)),
        ],
        out_specs=[pl.BlockSpec((1, M, D_MEM_ATTN), lambda b: (b, 0, 0))] * 2,
        out_shape=[jax.ShapeDtypeStruct((B, M, D_MEM_ATTN), BF16)] * 2,
        compiler_params=pltpu.CompilerParams(dimension_semantics=("parallel",),
                                             vmem_limit_bytes=VMEM_LIMIT),
        name="memkv",
    )(mem, g, wk_bf, wv_bf, kg)


def _split_bf16(a):
    hi = a.astype(BF16)
    return hi, (a - hi.astype(F32)).astype(BF16)


def _router_logits(h, wrh_ref, wrl_ref, br_ref):
    h_hi, h_lo = _split_bf16(h)
    dn = (((1,), (1,)), ((), ()))
    return (lax.dot_general(wrh_ref[...], h_hi, dn, preferred_element_type=F32)
            + lax.dot_general(wrh_ref[...], h_lo, dn, preferred_element_type=F32)
            + lax.dot_general(wrl_ref[...], h_hi, dn, preferred_element_type=F32)) + br_ref[...]


def _mix_kernel(x_ref, yh_ref, ya_ref, gh_ref, gmat_ref, wo_ref, gx_ref, wq_ref, qg_ref,
                km_ref, vm_ref, wom_ref, gmoe_ref, wrh_ref, wrl_ref, br_ref, tri_ref,
                x2_ref, route_ref, cnt_ref):
    x = x_ref[...]
    tm = x.shape[0]
    yh = yh_ref[...].astype(F32)
    ms = jnp.dot((yh * yh).astype(BF16), gmat_ref[...], preferred_element_type=F32)
    yh = yh * lax.rsqrt(ms + EPS) * gh_ref[...]
    x1 = (x
          + jnp.dot(yh.astype(BF16), wo_ref[:D_HYENA, :], preferred_element_type=F32)
          + jnp.dot(ya_ref[...], wo_ref[D_HYENA:, :], preferred_element_type=F32))
    h2 = (_rms(x1) * gx_ref[...]).astype(BF16)
    qm = jnp.dot(h2, wq_ref[...], preferred_element_type=F32)
    heads = []
    for i in range(MEM_HEADS):
        sl = slice(i * MEM_HEAD_DIM, (i + 1) * MEM_HEAD_DIM)
        qn = (_rms(qm[:, sl]) * qg_ref[...] * (MEM_HEAD_DIM ** -0.5)).astype(BF16)
        s = lax.dot_general(qn, km_ref[0, :, sl], (((1,), (1,)), ((), ())), preferred_element_type=F32)
        p = jnp.exp(s - jnp.max(s, axis=-1, keepdims=True))
        o = jnp.dot(p.astype(BF16), vm_ref[0, :, sl], preferred_element_type=F32)
        heads.append(o / jnp.sum(p, axis=-1, keepdims=True))
    om = jnp.concatenate(heads, axis=1).astype(BF16)
    x2 = x1 + jnp.dot(om, wom_ref[...], preferred_element_type=F32)
    for j in range(SLAB):
        x2_ref[pl.ds(j, tm, stride=SLAB), :] = x2[:, j * LANE:(j + 1) * LANE]

    lt = _router_logits(_rms(x2) * gmoe_ref[...], wrh_ref, wrl_ref, br_ref)
    g = [lt[i:i + 1, :] for i in range(N_GROUPS)]
    gmax = jnp.maximum(jnp.maximum(g[0], g[1]), jnp.maximum(g[2], g[3]))
    gidx = jnp.where(g[0] == gmax, 0, jnp.where(g[1] == gmax, 1, jnp.where(g[2] == gmax, 2, 3)))
    sel = []
    for j in range(EXPERTS_PER_GROUP):
        rows = [lt[N_GROUPS + EXPERTS_PER_GROUP * i + j:N_GROUPS + EXPERTS_PER_GROUP * i + j + 1, :]
                for i in range(N_GROUPS)]
        sel.append(jnp.where(gidx == 0, rows[0], jnp.where(gidx == 1, rows[1],
                                                            jnp.where(gidx == 2, rows[2], rows[3]))))
    def first_argmax(vals):
        best = jnp.maximum(jnp.maximum(vals[0], vals[1]), jnp.maximum(vals[2], vals[3]))
        return jnp.where(vals[0] == best, 0, jnp.where(vals[1] == best, 1, jnp.where(vals[2] == best, 2, 3)))

    a = first_argmax(sel)
    b = first_argmax([jnp.where(a == j, -jnp.inf, sel[j]) for j in range(EXPERTS_PER_GROUP)])
    lo, hi = jnp.minimum(a, b), jnp.maximum(a, b)
    pair = jnp.where(lo == 0, hi - 1, jnp.where(lo == 1, hi + 1, 5))
    cls = gidx * len(PAIRS) + pair

    rows_i = lax.broadcasted_iota(jnp.int32, (ROUTE_ROWS, tm), 0)
    onehot = rows_i == cls
    prefix = jnp.dot(jnp.where(onehot, 1.0, 0.0).astype(BF16), tri_ref[...], preferred_element_type=F32)
    rank = jnp.sum(jnp.where(onehot, prefix, 0.0), axis=0, keepdims=True) - 1.0
    rr = lax.broadcasted_iota(jnp.int32, (8, tm), 0)
    route_ref[...] = jnp.where(rr == 0, cls.astype(F32), jnp.where(rr == 1, rank, 0.0))
    cnt_ref[0] = prefix[:, tm - LANE:]


def _mix(x, y_hy, y_at, gh, wo_bf, gx, wq_bf, qg, km, vm, wom_bf, gmoe, wr_hi, wr_lo, br, tm):
    B, L, D = x.shape
    nt = L // tm
    T = B * L
    M = km.shape[1]
    gmat = _group_sumsq_matrix(D_HYENA, D_HYENA // HYENA_GROUPS)
    tri = jnp.asarray(np.triu(np.ones((tm, tm), np.float32)), dtype=BF16)
    const = lambda shape: pl.BlockSpec(shape, lambda i: (0,) * len(shape))
    return pl.pallas_call(
        _mix_kernel,
        grid=(B * nt,),
        in_specs=[
            pl.BlockSpec((tm, D), lambda i: (i, 0)),
            pl.BlockSpec((tm, D_HYENA), lambda i: (i, 0)),
            pl.BlockSpec((tm, D_ATTN), lambda i: (i, 0)),
            const((1, D_HYENA)), const((D_HYENA, D_HYENA)), const((D_HYENA + D_ATTN, D)),
            const((1, D)), const((D, D_MEM_ATTN)), const((1, MEM_HEAD_DIM)),
            pl.BlockSpec((1, M, D_MEM_ATTN), lambda i: (i // nt, 0, 0)),
            pl.BlockSpec((1, M, D_MEM_ATTN), lambda i: (i // nt, 0, 0)),
            const((D_MEM_ATTN, D)), const((1, D)),
            const((ROUTE_ROWS, D)), const((ROUTE_ROWS, D)), const((ROUTE_ROWS, 1)),
            const((tm, tm)),
        ],
        out_specs=[
            pl.BlockSpec((tm * SLAB, LANE), lambda i: (i, 0)),
            pl.BlockSpec((8, tm), lambda i: (0, i)),
            pl.BlockSpec((1, ROUTE_ROWS, LANE), lambda i: (i, 0, 0)),
        ],
        out_shape=[
            jax.ShapeDtypeStruct((T * SLAB, LANE), F32),
            jax.ShapeDtypeStruct((8, T), F32),
            jax.ShapeDtypeStruct((B * nt, ROUTE_ROWS, LANE), F32),
        ],
        compiler_params=pltpu.CompilerParams(dimension_semantics=("parallel",),
                                             vmem_limit_bytes=VMEM_LIMIT),
        name="mix_mem_route",
    )(x.reshape(T, D), y_hy, y_at.reshape(T, D_ATTN), gh, gmat, wo_bf, gx, wq_bf, qg,
      km, vm, wom_bf, gmoe, wr_hi, wr_lo, br, tri)


def _slab_copy(src_ref, src_tok, dst_ref, dst_tok, n, sem):
    s0 = pl.multiple_of(src_tok * SLAB, SLAB)
    d0 = pl.multiple_of(dst_tok * SLAB, SLAB)
    return pltpu.make_async_copy(src_ref.at[pl.ds(s0, n * SLAB)], dst_ref.at[pl.ds(d0, n * SLAB)], sem)


def _index_spec(rows):
    return pl.BlockSpec((1, 1, rows), lambda i: (i, 0, 0), memory_space=pltpu.SMEM)


def _scatter_rows_kernel(idx_ref, src_ref, init_ref, dst_ref, sem, *, rows):
    del init_ref

    def issue(r, carry):
        _slab_copy(src_ref, r, dst_ref, idx_ref[0, 0, r], 1, sem).start()
        return carry

    lax.fori_loop(0, rows, issue, 0, unroll=DMA_UNROLL)
    _slab_copy(src_ref, 0, dst_ref, 0, rows, sem).wait()


def _scatter_rows(src, dest, n_out, rows):
    T = dest.shape[0]
    init = jnp.zeros((n_out * SLAB, LANE), src.dtype)
    return pl.pallas_call(
        functools.partial(_scatter_rows_kernel, rows=rows),
        grid=(T // rows,),
        in_specs=[_index_spec(rows), pl.BlockSpec((rows * SLAB, LANE), lambda i: (i, 0)),
                  pl.BlockSpec(memory_space=pl.ANY)],
        out_specs=pl.BlockSpec(memory_space=pl.ANY),
        scratch_shapes=[pltpu.SemaphoreType.DMA(())],
        out_shape=jax.ShapeDtypeStruct((n_out * SLAB, LANE), src.dtype),
        input_output_aliases={2: 0},
        compiler_params=pltpu.CompilerParams(dimension_semantics=("arbitrary",),
                                             vmem_limit_bytes=VMEM_LIMIT),
        name="scatter_rows",
    )(dest.reshape(T // rows, 1, rows), src, init)


def _gather_rows_kernel(idx_ref, idx_next_ref, ys_ref, o_ref, buf_ref, sem):
    i = pl.program_id(0)
    tm = o_ref.shape[0]

    def issue(tile_idx_ref, slot):
        def body(r, carry):
            _slab_copy(ys_ref, tile_idx_ref[0, 0, r], buf_ref.at[slot], r, 1, sem.at[slot]).start()
            return carry

        lax.fori_loop(0, tm, body, 0, unroll=DMA_UNROLL)

    def step(slot):
        @pl.when(i == 0)
        def _():
            issue(idx_ref, slot)

        @pl.when(i + 1 < pl.num_programs(0))
        def _():
            issue(idx_next_ref, 1 - slot)

        _slab_copy(ys_ref, 0, buf_ref.at[slot], 0, tm, sem.at[slot]).wait()
        o_ref[...] = jnp.concatenate([buf_ref[slot, pl.ds(j, tm, stride=SLAB), :] for j in range(SLAB)],
                                     axis=1)

    for slot in range(2):
        pl.when(i % 2 == slot)(functools.partial(step, slot))


def _gather_rows(ys, dest, tm):
    T = dest.shape[0]
    D = SLAB * LANE
    n = T // tm
    idx = dest.reshape(n, 1, tm)
    next_spec = pl.BlockSpec((1, 1, tm), lambda i: (jnp.minimum(i + 1, n - 1), 0, 0), memory_space=pltpu.SMEM)
    return pl.pallas_call(
        _gather_rows_kernel,
        grid=(n,),
        in_specs=[_index_spec(tm), next_spec, pl.BlockSpec(memory_space=pl.ANY)],
        out_specs=pl.BlockSpec((tm, D), lambda i: (i, 0)),
        scratch_shapes=[pltpu.VMEM((2, tm * SLAB, LANE), F32), pltpu.SemaphoreType.DMA((2,))],
        out_shape=jax.ShapeDtypeStruct((T, D), F32),
        compiler_params=pltpu.CompilerParams(dimension_semantics=("arbitrary",),
                                             vmem_limit_bytes=VMEM_LIMIT),
        name="gather_rows",
    )(idx, idx, ys)


def _moe_kernel(lo_ref, hi_ref, valid_ref, xs_ref, gmoe_ref, wrh_ref, br_ref,
                gu_lo_ref, dn_lo_ref, gu_hi_ref, dn_hi_ref, o_ref):
    i = pl.program_id(0)
    ts = o_ref.shape[0] // SLAB

    @pl.when(valid_ref[i] > 0)
    def _():
        x2 = jnp.concatenate([xs_ref[pl.ds(j, ts, stride=SLAB), :] for j in range(SLAB)], axis=1)
        h = (_rms(x2) * gmoe_ref[...]).astype(BF16)
        lt = lax.dot_general(wrh_ref[...], h, (((1,), (1,)), ((), ())),
                             preferred_element_type=F32) + br_ref[...]
        rows = lax.broadcasted_iota(jnp.int32, lt.shape, 0)
        e_lo, e_hi = lo_ref[i], hi_ref[i]
        pick = lambda r: jnp.sum(jnp.where(rows == r, lt, 0.0), axis=0, keepdims=True)
        is_grp = rows < N_GROUPS
        gmax = jnp.max(jnp.where(is_grp, lt, -jnp.inf), axis=0, keepdims=True)
        p_grp = (jnp.exp(pick(e_lo // EXPERTS_PER_GROUP) - gmax)
                 / jnp.sum(jnp.where(is_grp, jnp.exp(lt - gmax), 0.0), axis=0, keepdims=True))
        s_lo, s_hi = pick(N_GROUPS + e_lo), pick(N_GROUPS + e_hi)
        smax = jnp.maximum(s_lo, s_hi)
        x_lo, x_hi = jnp.exp(s_lo - smax), jnp.exp(s_hi - smax)
        w_lo = p_grp * x_lo / (x_lo + x_hi)
        w_hi = p_grp * x_hi / (x_lo + x_hi)
        rw = lax.broadcasted_iota(jnp.int32, (LANE, ts), 0)
        gates = jnp.where(rw == 0, w_lo, jnp.where(rw == 1, w_hi, 0.0)).T
        y = x2
        for col, gu_ref, dn_ref in ((0, gu_lo_ref, dn_lo_ref), (1, gu_hi_ref, dn_hi_ref)):
            gu = jnp.dot(h, gu_ref[0], preferred_element_type=F32)
            act = jax.nn.silu(gu[:, :D_EXPERT]) * gu[:, D_EXPERT:]
            y = y + gates[:, col:col + 1] * jnp.dot(act.astype(BF16), dn_ref[0], preferred_element_type=F32)
        for j in range(SLAB):
            o_ref[pl.ds(j, ts, stride=SLAB), :] = y[:, j * LANE:(j + 1) * LANE]

    @pl.when(valid_ref[i] == 0)
    def _():
        o_ref[...] = jnp.zeros_like(o_ref)


def _moe(xs, tile_lo, tile_hi, tile_valid, gmoe, wr_hi, br, w_gu_bf, w_dn_bf, ts):
    n_tiles = tile_lo.shape[0]
    D = D_MODEL
    const = lambda shape: pl.BlockSpec(shape, lambda i, lo, hi, va: (0,) * len(shape))
    return pl.pallas_call(
        _moe_kernel,
        grid_spec=pltpu.PrefetchScalarGridSpec(
            num_scalar_prefetch=3,
            grid=(n_tiles,),
            in_specs=[
                pl.BlockSpec((ts * SLAB, LANE), lambda i, lo, hi, va: (i, 0)),
                const((1, D)), const((ROUTE_ROWS, D)), const((ROUTE_ROWS, 1)),
                pl.BlockSpec((1, D, 2 * D_EXPERT), lambda i, lo, hi, va: (lo[i], 0, 0)),
                pl.BlockSpec((1, D_EXPERT, D), lambda i, lo, hi, va: (lo[i], 0, 0)),
                pl.BlockSpec((1, D, 2 * D_EXPERT), lambda i, lo, hi, va: (hi[i], 0, 0)),
                pl.BlockSpec((1, D_EXPERT, D), lambda i, lo, hi, va: (hi[i], 0, 0)),
            ],
            out_specs=pl.BlockSpec((ts * SLAB, LANE), lambda i, lo, hi, va: (i, 0)),
        ),
        out_shape=jax.ShapeDtypeStruct((n_tiles * ts * SLAB, LANE), F32),
        compiler_params=pltpu.CompilerParams(dimension_semantics=("arbitrary",),
                                             vmem_limit_bytes=VMEM_LIMIT),
        name="moe_experts",
    )(tile_lo, tile_hi, tile_valid, xs, gmoe, wr_hi, br, w_gu_bf, w_dn_bf, w_gu_bf, w_dn_bf)


def _route_tables(route, counts, tm, ts):
    T = route.shape[1]
    cls = route[0].astype(jnp.int32)
    rank = route[1].astype(jnp.int32)
    cnt = counts[:, :N_CLASSES, LANE - 1].astype(jnp.int32)
    total = jnp.sum(cnt, axis=0)
    tiles_c = (total + ts - 1) // ts
    tile_start = jnp.cumsum(tiles_c) - tiles_c
    before = jnp.cumsum(cnt, axis=0) - cnt
    base = tile_start[None, :] * ts + before
    classes = jnp.arange(N_CLASSES, dtype=jnp.int32)
    pick = cls.reshape(-1, tm)[:, :, None] == classes[None, None, :]
    dest = jnp.sum(jnp.where(pick, base[:, None, :], 0), axis=-1).reshape(T) + rank
    n_tiles = T // ts + N_CLASSES
    tile_id = jnp.arange(n_tiles, dtype=jnp.int32)
    used = jnp.sum(tiles_c)
    ends = jnp.cumsum(tiles_c)
    tcls = jnp.sum(jnp.minimum(tile_id, used - 1)[:, None] >= ends[None, :], axis=1).astype(jnp.int32)
    tcls = jnp.minimum(tcls, N_CLASSES - 1)
    grp, pr = tcls // len(PAIRS), tcls % len(PAIRS)
    pair_lo = (pr >= 3).astype(jnp.int32) + (pr >= 5).astype(jnp.int32)
    pair_hi = jnp.where(pr < 3, pr + 1, jnp.where(pr < 5, pr - 1, 3))
    tile_lo = grp * EXPERTS_PER_GROUP + pair_lo
    tile_hi = grp * EXPERTS_PER_GROUP + pair_hi
    return dest, tile_lo, tile_hi, (tile_id < used).astype(jnp.int32)


def _moe_block(x2s, route, counts, gmoe, wr_hi, br, w_gu_bf, w_dn_bf, tm):
    T = route.shape[1]
    ts = _pick_tile(T, 256)
    dest, tile_lo, tile_hi, tile_valid = _route_tables(route, counts, tm, ts)
    n_tiles = T // ts + N_CLASSES
    xs = _scatter_rows(x2s, dest, n_tiles * ts, _pick_tile(T, 1024))
    ys = _moe(xs, tile_lo, tile_hi, tile_valid, gmoe, wr_hi, br, w_gu_bf, w_dn_bf, ts)
    return _gather_rows(ys, dest, tm)


def _pick_tile(n, pref):
    t = min(pref, n)
    while n % t:
        t //= 2
    return t


def kernel(x, mem, norm_mix, w_in, hyena_short_w, hyena_short_b, filt_w1, filt_b1, filt_w2, filt_b2, filt_w3, filt_sin_freq, hyena_skip, attn_q_norm, attn_k_norm, out_norm_hyena, out_norm_attn, w_out, norm_xattn, norm_mem, w_q_mem, w_k_mem, w_v_mem, mem_q_norm, mem_k_norm, w_o_mem, norm_moe, w_router_grp, b_router_grp, w_router_exp, b_router_exp, w_gate, w_up, w_down):
    B, L, D = x.shape
    T = B * L
    NR = L // GRID_W
    depth = norm_mix.shape[0]
    tm = _pick_tile(L, 1024)
    tq = _pick_tile(L, 1024)
    tk = _pick_tile(L, 256)
    cos_t, sin_t = _rope_tables(L)

    def to_column_major(a):
        return a.reshape(B, NR, GRID_W, -1).transpose(0, 2, 1, 3).reshape(T, -1)

    def to_row_major(a):
        return a.reshape(B, GRID_W, NR, -1).transpose(0, 2, 1, 3).reshape(T, -1)

    for i in range(depth):
        qkg = jnp.concatenate([jnp.tile(attn_q_norm[i], N_Q_HEADS), jnp.tile(attn_k_norm[i], N_KV_HEADS)])[None, :]
        u, q, k, v = _inproj(x, norm_mix[i][None, :], w_in[i].astype(BF16), cos_t, sin_t, qkg, tm)
        y_hy = to_row_major(_hyena(to_column_major(u), hyena_short_w[i], hyena_short_b[i], hyena_skip[i],
                                   filt_w1[i], filt_b1[i], filt_w2[i], filt_b2[i], filt_w3[i],
                                   filt_sin_freq[i], B, L))
        y_at = _attention(q, k, v, out_norm_attn[i][None, :], tq, tk)
        km, vm = _memkv(mem, norm_mem[i][None, :], w_k_mem[i].astype(BF16), w_v_mem[i].astype(BF16),
                        mem_k_norm[i][None, :])
        wr = jnp.concatenate([w_router_grp[i], w_router_exp[i]], axis=1).T
        wr = jnp.pad(wr, ((0, ROUTE_ROWS - wr.shape[0]), (0, 0)))
        wr_hi, wr_lo = _split_bf16(wr)
        br = jnp.pad(jnp.concatenate([b_router_grp[i], b_router_exp[i]]), (0, ROUTE_ROWS - N_GROUPS - N_EXPERTS))
        gmoe = norm_moe[i][None, :]
        x2s, route, counts = _mix(x, y_hy, y_at, out_norm_hyena[i][None, :], w_out[i].astype(BF16),
                                  norm_xattn[i][None, :], w_q_mem[i].astype(BF16), mem_q_norm[i][None, :],
                                  km, vm, w_o_mem[i].astype(BF16), gmoe, wr_hi, wr_lo, br[:, None], tm)
        w_gu = jnp.concatenate([w_gate[i], w_up[i]], axis=-1).astype(BF16)
        x = _moe_block(x2s, route, counts, gmoe, wr_hi, br[:, None], w_gu, w_down[i].astype(BF16),
                       tm).reshape(B, L, D)
    return x
```

```python
import functools
import math

import jax
import jax.numpy as jnp
import numpy as np
from jax import lax
from jax.experimental import pallas as pl
from jax.experimental.pallas import tpu as pltpu

F32 = jnp.float32
BF16 = jnp.bfloat16

D_MODEL = 1024
GRID_W = 64
D_HYENA = 512
HYENA_GROUPS = 8
HYENA_ORDER = 2
SHORT_CONV = 3
FILTER_BANDS = 16
FILTER_FF = 64
DECAY_TARGET = 1e-2
FAST_DECAY_PCT = 0.3
SLOW_DECAY_PCT = 1.5
N_Q_HEADS = 8
N_KV_HEADS = 2
HEAD_DIM = 64
D_ATTN = N_Q_HEADS * HEAD_DIM
D_KV = N_KV_HEADS * HEAD_DIM
ROPE_THETA = 10000.0
D_U = (HYENA_ORDER + 1) * D_HYENA
D_IN = D_U + D_ATTN + 2 * D_KV
D_QK = D_ATTN + D_KV
MEM_HEADS = 4
MEM_HEAD_DIM = 128
D_MEM_ATTN = MEM_HEADS * MEM_HEAD_DIM
N_GROUPS = 4
EXPERTS_PER_GROUP = 4
N_EXPERTS = N_GROUPS * EXPERTS_PER_GROUP
D_EXPERT = 512
EPS = 1e-6

PAIRS = ((0, 1), (0, 2), (0, 3), (1, 2), (1, 3), (2, 3))
N_CLASSES = N_GROUPS * len(PAIRS)
ROUTE_ROWS = 32
LANE = 128
SUBLANE = 8
SLAB = SUBLANE
VMEM_LIMIT = 56 * 1024 * 1024
HY_CB = 256
HY_PAD = 8
HY_UNROLL = 8
HY_GROUP = 4
DMA_UNROLL = 8
LOG2E = math.log2(math.e)


def _rms(x):
    return x * lax.rsqrt(jnp.mean(x * x, axis=-1, keepdims=True) + EPS)


def _group_sumsq_matrix(width, group):
    idx = np.arange(width) // group
    return jnp.asarray((idx[:, None] == idx[None, :]).astype(np.float32) / group, dtype=BF16)


def _hy_pitch(n1):
    return 4 * n1 + HY_PAD


def _inproj_kernel(x_ref, g_ref, w_ref, cos_ref, sin_ref, qkg_ref, gmat_ref,
                   u_ref, q_ref, k_ref, v_ref):
    x = x_ref[...]
    tm = x.shape[0]
    h = _rms(x) * g_ref[...]
    p = jnp.dot(h.astype(BF16), w_ref[...], preferred_element_type=F32)
    u_ref[...] = p[:, :D_U].astype(BF16)
    qk = p[:, D_U:D_U + D_QK]
    ms = jnp.dot((qk * qk).astype(BF16), gmat_ref[...], preferred_element_type=F32)
    qk = qk * lax.rsqrt(ms + EPS) * qkg_ref[...]
    reps = D_QK // LANE
    cos = jnp.concatenate([cos_ref[...]] * reps, axis=1)
    sin = jnp.concatenate([sin_ref[...]] * reps, axis=1)
    lane = lax.broadcasted_iota(jnp.int32, (tm, D_QK), 1)
    partner = jnp.where((lane % 32) < 16, pltpu.roll(qk, D_QK - 16, 1), pltpu.roll(qk, 16, 1))
    qk = qk * cos + partner * sin
    for hd in range(N_Q_HEADS):
        q_ref[0, hd] = (qk[:, hd * HEAD_DIM:(hd + 1) * HEAD_DIM] * (HEAD_DIM ** -0.5 * LOG2E)).astype(BF16)
    for hd in range(N_KV_HEADS):
        k_ref[0, hd] = qk[:, D_ATTN + hd * HEAD_DIM:D_ATTN + (hd + 1) * HEAD_DIM].astype(BF16)
    vv = p[:, D_U + D_QK:]
    lane_v = lax.broadcasted_iota(jnp.int32, (tm, LANE), 1)
    ones_col = jnp.where(lane_v == HEAD_DIM, 1.0, 0.0)
    v_ref[0, 0] = jnp.where(lane_v < HEAD_DIM, vv, ones_col).astype(BF16)
    v_ref[0, 1] = jnp.where(lane_v < HEAD_DIM, pltpu.roll(vv, HEAD_DIM, 1), ones_col).astype(BF16)


def _inproj(x, g, w_bf, cos_t, sin_t, qkg, tm):
    B, L, D = x.shape
    nt = L // tm
    gmat = _group_sumsq_matrix(D_QK, HEAD_DIM)
    return pl.pallas_call(
        _inproj_kernel,
        grid=(B * nt,),
        in_specs=[
            pl.BlockSpec((tm, D), lambda i: (i, 0)),
            pl.BlockSpec((1, D), lambda i: (0, 0)),
            pl.BlockSpec((D, D_IN), lambda i: (0, 0)),
            pl.BlockSpec((tm, LANE), lambda i: (i % nt, 0)),
            pl.BlockSpec((tm, LANE), lambda i: (i % nt, 0)),
            pl.BlockSpec((1, D_QK), lambda i: (0, 0)),
            pl.BlockSpec((D_QK, D_QK), lambda i: (0, 0)),
        ],
        out_specs=[
            pl.BlockSpec((tm, D_U), lambda i: (i, 0)),
            pl.BlockSpec((1, N_Q_HEADS, tm, HEAD_DIM), lambda i: (i // nt, 0, i % nt, 0)),
            pl.BlockSpec((1, N_KV_HEADS, tm, HEAD_DIM), lambda i: (i // nt, 0, i % nt, 0)),
            pl.BlockSpec((1, N_KV_HEADS, tm, LANE), lambda i: (i // nt, 0, i % nt, 0)),
        ],
        out_shape=[
            jax.ShapeDtypeStruct((B * L, D_U), BF16),
            jax.ShapeDtypeStruct((B, N_Q_HEADS, L, HEAD_DIM), BF16),
            jax.ShapeDtypeStruct((B, N_KV_HEADS, L, HEAD_DIM), BF16),
            jax.ShapeDtypeStruct((B, N_KV_HEADS, L, LANE), BF16),
        ],
        compiler_params=pltpu.CompilerParams(dimension_semantics=("parallel",),
                                             vmem_limit_bytes=VMEM_LIMIT),
        name="inproj",
    )(x.reshape(B * L, D), g, w_bf, cos_t, sin_t, qkg, gmat)


def _rope_tables(L):
    p = np.arange(L)
    row, col = p // GRID_W, p % GRID_W
    d = np.arange(LANE) % HEAD_DIM
    sec, r = d // 32, d % 32
    inv = (ROPE_THETA ** (-(r % 16).astype(np.float64) / 16.0)).astype(np.float32)
    pos = np.where(sec[None, :] == 0, row[:, None], col[:, None]).astype(np.float32)
    ang = pos * inv[None, :]
    sign = np.where(r < 16, -1.0, 1.0)[None, :]
    return jnp.asarray(np.cos(ang), F32), jnp.asarray(np.sin(ang) * sign, F32)


def _real_embed(m):
    return np.block([[m.real, -m.imag], [m.imag, m.real]])


@functools.lru_cache(maxsize=None)
def _dft_tables(n1, n2):
    L = n1 * n2
    i1, i2 = np.arange(n1), np.arange(n2)
    om = np.exp(-1j * np.pi / L)
    f1 = np.exp(-2j * np.pi * np.outer(i1, i1) / n1)
    tw = np.exp(-2j * np.pi * np.outer(i1, i2) / L)
    fwd1, fwd1_real = [], []
    for t in i2:
        ev = tw[:, t][:, None] * f1
        od = (om ** t) * tw[:, t][:, None] * f1 * (om ** (n2 * i1))[None, :]
        fwd1.append(np.concatenate([_real_embed(ev), _real_embed(od)], axis=0))
        z = np.zeros((n1, n1))
        fwd1_real.append(np.block([[ev.real, z], [ev.imag, z], [z, od.real], [z, od.imag]]))
    f2 = _real_embed(np.exp(-2j * np.pi * np.outer(i2, i2) / n2))
    f2i = np.exp(2j * np.pi * np.outer(i2, i2) / n2)
    inv_e = np.stack([_real_embed(np.conj(tw[k, :])[:, None] * f2i) for k in i1])
    inv_o = np.stack([_real_embed((om ** (-i2))[:, None] * np.conj(tw[k, :])[:, None] * f2i) for k in i1])
    f1i = np.exp(2j * np.pi * np.outer(i1, i1) / n1)
    inv2 = np.concatenate([_real_embed(f1i), _real_embed((om ** (-n2 * i1))[:, None] * f1i)], axis=1) / (2 * L)
    as_f32 = lambda a: np.asarray(a, np.float32)
    return {"fwd1": as_f32(np.stack(fwd1)), "fwd1_real": as_f32(np.stack(fwd1_real)), "fwd2": as_f32(f2),
            "inv_e": as_f32(inv_e), "inv_o": as_f32(inv_o), "inv2": as_f32(inv2)}


def _aligned(start, align):
    return start if isinstance(start, int) else pl.multiple_of(start, align)


def _conv_block(ref, w_ref, b_ref, base, t2, n1, n2, edge):
    def blk(i):
        return ref[pl.ds(_aligned(base + i * n1, n1), n1), :].astype(F32)

    cur = blk(t2)
    if not edge:
        prev, nxt = blk(jnp.maximum(t2 - 1, 0)), blk(jnp.minimum(t2 + 1, n2 - 1))
    else:
        row = lax.broadcasted_iota(jnp.int32, cur.shape, 0)
        if t2 == 0:
            prev = jnp.where(row == 0, 0.0, pltpu.roll(blk(n2 - 1), 1, 0))
            nxt = blk(1)
        else:
            prev = blk(n2 - 2)
            nxt = jnp.where(row == n1 - 1, 0.0, pltpu.roll(blk(0), n1 - 1, 0))
    return prev * w_ref[0:1, :] + cur * w_ref[1:2, :] + nxt * w_ref[2:3, :] + b_ref[...]


def _plain_block(ref, base, t2, n1):
    return ref[pl.ds(_aligned(base + t2 * n1, n1), n1), :].astype(F32)


def _store_slabs(a_ref, rows, val):
    for s in range(val.shape[1] // LANE):
        a_ref[s, rows, :] = val[:, s * LANE:(s + 1) * LANE]


def _load_slabs(a_ref, rows):
    return jnp.concatenate([a_ref[s, rows, :] for s in range(a_ref.shape[0])], axis=1)


def _forward_stage1(load_stacked, fwd1_ref, a_ref, n1, n2, redo_edges):
    pitch = _hy_pitch(n1)

    def body(t2, edge):
        a = jnp.dot(fwd1_ref[t2], load_stacked(t2, edge).astype(BF16), preferred_element_type=F32)
        _store_slabs(a_ref, pl.ds(_aligned(t2 * pitch, SUBLANE), 4 * n1), a)

    def loop_body(t2, carry):
        body(t2, False)
        return carry

    lax.fori_loop(0, n2, loop_body, 0, unroll=HY_UNROLL)
    if redo_edges:
        body(0, True)
        body(n2 - 1, True)


def _forward_stage2(a_ref, fwd2_ref, k1s, n1, n2):
    pitch = _hy_pitch(n1)
    cols = []
    for k1 in k1s:
        for half in range(2):
            cols.append(jnp.concatenate(
                [_load_slabs(a_ref, pl.ds((2 * half + part) * n1 + k1, n2, stride=pitch)) for part in range(2)],
                axis=0))
    x = jnp.dot(fwd2_ref[...], jnp.concatenate(cols, axis=1).astype(BF16), preferred_element_type=F32)
    w2 = x.shape[1] // len(k1s)
    return [(x[:n2, i * w2:(i + 1) * w2], x[n2:, i * w2:(i + 1) * w2]) for i in range(len(k1s))]


def _hyena_kernel(*refs, n1, n2, conv_z):
    L = n1 * n2
    if conv_z:
        (z_ref, zw_ref, zb_ref, g_ref, gw_ref, gb_ref, ker_ref, kei_ref, kor_ref, koi_ref,
         skip_ref, fwd1_ref, fwd2_ref, inve_ref, invo_ref, inv2_ref, o_ref, a_ref) = refs
        load_z = lambda part, t2, edge: _conv_block(z_ref, zw_ref, zb_ref, part * L, t2, n1, n2, edge)
    else:
        (z_ref, g_ref, gw_ref, gb_ref, ker_ref, kei_ref, kor_ref, koi_ref,
         skip_ref, fwd1_ref, fwd2_ref, inve_ref, invo_ref, inv2_ref, o_ref, a_ref) = refs
        load_z = lambda part, t2, edge: _plain_block(z_ref, part * L, t2, n1)
    cb = z_ref.shape[1]
    pitch = _hy_pitch(n1)

    _forward_stage1(lambda t2, edge: jnp.concatenate([load_z(0, t2, edge), load_z(1, t2, edge)], axis=0),
                    fwd1_ref, a_ref, n1, n2, redo_edges=conv_z)

    skip2 = jnp.concatenate([skip_ref[...], skip_ref[...]], axis=1)
    group = HY_GROUP if n1 % HY_GROUP == 0 else 1

    def freq_body(i, carry):
        k1s = [i * group + j for j in range(group)]
        results = []
        for k1, (xr, xi) in zip(k1s, _forward_stage2(a_ref, fwd2_ref, k1s, n1, n2)):
            rows = pl.ds(pl.multiple_of(k1 * n2, n2), n2)
            kr = jnp.concatenate([ker_ref[rows, :], kor_ref[rows, :]], axis=1).astype(F32) + skip2
            ki = jnp.concatenate([kei_ref[rows, :], koi_ref[rows, :]], axis=1).astype(F32)
            yr = xr * kr - xi * ki
            yi = xr * ki + xi * kr
            ye = jnp.concatenate([yr[:, :cb], yi[:, :cb]], axis=0).astype(BF16)
            yo = jnp.concatenate([yr[:, cb:], yi[:, cb:]], axis=0).astype(BF16)
            be = jnp.dot(inve_ref[k1], ye, preferred_element_type=F32)
            bo = jnp.dot(invo_ref[k1], yo, preferred_element_type=F32)
            results.append((be[:n2], be[n2:], bo[:n2], bo[n2:]))
        for k1, quarters in zip(k1s, results):
            for q, val in enumerate(quarters):
                _store_slabs(a_ref, pl.ds(q * n1 + k1, n2, stride=pitch), val)
        return carry

    lax.fori_loop(0, n1 // group, freq_body, 0)

    def time_block(t2, edge):
        blk = _load_slabs(a_ref, pl.ds(_aligned(t2 * pitch, SUBLANE), 4 * n1))
        y = jnp.dot(inv2_ref[...], blk.astype(BF16), preferred_element_type=F32)
        for part in range(2):
            gate = _conv_block(g_ref, gw_ref, gb_ref, part * L, t2, n1, n2, edge)
            o_ref[pl.ds(_aligned(part * L + t2 * n1, n1), n1), :] = (
                gate * y[part * n1:(part + 1) * n1]).astype(o_ref.dtype)

    def time_body(t2, carry):
        time_block(t2, False)
        return carry

    lax.fori_loop(0, n2, time_body, 0, unroll=HY_UNROLL)
    time_block(0, True)
    time_block(n2 - 1, True)


def _hyena_order(z, z_col0, u, gate_col0, short_w, short_b, spectra, spec_col0, skip, tabs, B, L, conv_z):
    n2 = GRID_W
    n1 = L // n2
    cb = HY_CB
    ncb = D_HYENA // cb
    pitch = _hy_pitch(n1)
    col = lambda off: (lambda c, p: (0, off + c))
    pair = lambda off: (lambda c, p: (p, off + c))
    one = pl.Buffered(1)
    const3 = lambda a: pl.BlockSpec(a.shape, lambda c, p: (0, 0, 0))
    const2 = lambda a: pl.BlockSpec(a.shape, lambda c, p: (0, 0))
    in_specs = [pl.BlockSpec((2 * L, cb), pair(z_col0), pipeline_mode=one)]
    args = [z]
    if conv_z:
        in_specs += [pl.BlockSpec((SHORT_CONV, cb), col(z_col0)), pl.BlockSpec((1, cb), col(z_col0))]
        args += [short_w, short_b]
    in_specs += [pl.BlockSpec((2 * L, cb), pair(gate_col0), pipeline_mode=one),
                 pl.BlockSpec((SHORT_CONV, cb), col(gate_col0)), pl.BlockSpec((1, cb), col(gate_col0))]
    args += [u, short_w, short_b]
    in_specs += [pl.BlockSpec((L, cb), col(spec_col0), pipeline_mode=one)] * 4
    args += list(spectra)
    in_specs += [pl.BlockSpec((1, cb), lambda c, p: (0, c)),
                 const3(tabs["fwd1"]), const2(tabs["fwd2"]), const3(tabs["inv_e"]), const3(tabs["inv_o"]),
                 const2(tabs["inv2"])]
    args += [skip, tabs["fwd1"], tabs["fwd2"], tabs["inv_e"], tabs["inv_o"], tabs["inv2"]]
    return pl.pallas_call(
        functools.partial(_hyena_kernel, n1=n1, n2=n2, conv_z=conv_z),
        grid=(ncb, B // 2),
        in_specs=in_specs,
        out_specs=pl.BlockSpec((2 * L, cb), lambda c, p: (p, c)),
        out_shape=jax.ShapeDtypeStruct((B * L, D_HYENA), BF16),
        scratch_shapes=[pltpu.VMEM((cb // LANE, n2 * pitch, LANE), F32)],
        compiler_params=pltpu.CompilerParams(dimension_semantics=("arbitrary", "arbitrary"),
                                             vmem_limit_bytes=VMEM_LIMIT),
        name="hyena_order",
    )(*args)


def _spectra_kernel(kc_ref, kn_ref, fwd1_ref, fwd2_ref, er_ref, ei_ref, or_ref, oi_ref, a_ref, *, n1, n2):
    cb = kc_ref.shape[1]
    _forward_stage1(lambda t2, edge: jnp.concatenate([_plain_block(kc_ref, 0, t2, n1),
                                                     _plain_block(kn_ref, 0, t2, n1)], axis=0),
                    fwd1_ref, a_ref, n1, n2, redo_edges=False)

    def freq_body(k1, carry):
        (xr, xi), = _forward_stage2(a_ref, fwd2_ref, [k1], n1, n2)
        rows = pl.ds(pl.multiple_of(k1 * n2, n2), n2)
        er_ref[rows, :] = xr[:, :cb].astype(BF16)
        ei_ref[rows, :] = xi[:, :cb].astype(BF16)
        or_ref[rows, :] = xr[:, cb:].astype(BF16)
        oi_ref[rows, :] = xi[:, cb:].astype(BF16)
        return carry

    lax.fori_loop(0, n1, freq_body, 0, unroll=HY_UNROLL)


def _filter_spectra(kc, kn, tabs, L):
    n2 = GRID_W
    n1 = L // n2
    cb = HY_CB
    C = kc.shape[1]
    pitch = _hy_pitch(n1)
    return pl.pallas_call(
        functools.partial(_spectra_kernel, n1=n1, n2=n2),
        grid=(C // cb,),
        in_specs=[pl.BlockSpec((L, cb), lambda c: (0, c)), pl.BlockSpec((L, cb), lambda c: (0, c)),
                  pl.BlockSpec(tabs["fwd1_real"].shape, lambda c: (0, 0, 0)),
                  pl.BlockSpec(tabs["fwd2"].shape, lambda c: (0, 0))],
        out_specs=[pl.BlockSpec((L, cb), lambda c: (0, c))] * 4,
        out_shape=[jax.ShapeDtypeStruct((L, C), BF16)] * 4,
        scratch_shapes=[pltpu.VMEM((cb // LANE, n2 * pitch, LANE), F32)],
        compiler_params=pltpu.CompilerParams(dimension_semantics=("arbitrary",),
                                             vmem_limit_bytes=VMEM_LIMIT),
        name="filter_spectra",
    )(kc, kn, tabs["fwd1_real"], tabs["fwd2"])


def _filter_time_kernel(zf_ref, zr_ref, w1_ref, b1_ref, w2_ref, b2_ref, w3f_ref, w3b_ref, sf_ref,
                        dl_ref, kc_ref, kn_ref, hf_ref, hr_ref):
    hp = lax.Precision.HIGHEST

    def dot3(a, b):
        a_hi, a_lo = _split_bf16(a)
        b_hi, b_lo = _split_bf16(b)
        return (jnp.dot(a_hi, b_hi, preferred_element_type=F32) + jnp.dot(a_hi, b_lo, preferred_element_type=F32)
                + jnp.dot(a_lo, b_hi, preferred_element_type=F32))

    df = jnp.exp(-zf_ref[:, 0:1] * dl_ref[...])
    row = lax.broadcasted_iota(jnp.int32, df.shape, 0)
    dr = jnp.where(row > 0, jnp.exp(-zr_ref[:, 0:1] * dl_ref[...]), 0.0)

    @pl.when(pl.program_id(0) == 0)
    def _():
        for z_ref, h_ref in ((zf_ref, hf_ref), (zr_ref, hr_ref)):
            a = jnp.sin(sf_ref[0:1, :] * (jnp.dot(z_ref[...], w1_ref[...], precision=hp,
                                                  preferred_element_type=F32) + b1_ref[...]))
            h_ref[...] = jnp.sin(sf_ref[1:2, :] * (jnp.dot(a, w2_ref[...], precision=hp,
                                                           preferred_element_type=F32) + b2_ref[...]))

    fwd = dot3(hf_ref[...], w3f_ref[...]) * df
    bwd = dot3(hr_ref[...], w3b_ref[...]) * dr
    norm = (jnp.sum(jnp.abs(fwd), axis=0, keepdims=True) + jnp.sum(jnp.abs(bwd), axis=0, keepdims=True)
            + EPS)
    kc_ref[...] = (fwd + bwd) / norm
    kn_ref[...] = (fwd - bwd) / norm


def _filter_time(w1, b1, w2, b2, w3, sin_freq, L):
    NR = L // GRID_W
    f64 = np.float64
    p = np.arange(L)
    t_nat = (p % NR) * GRID_W + p // NR
    tl = np.linspace(0.0, 1.0, L)
    bands = np.linspace(1e-4, FILTER_BANDS - 1, FILTER_BANDS)
    wv = (2.0 * math.pi / L) * np.arange(L, dtype=f64)
    feats = np.concatenate([tl[:, None], np.cos(bands[None, :] * wv[:, None]),
                            np.sin(bands[None, :] * wv[:, None])], axis=1)
    deltas = np.abs(np.linspace(math.log(DECAY_TARGET) / SLOW_DECAY_PCT,
                                math.log(DECAY_TARGET) / FAST_DECAY_PCT, D_HYENA))
    t_rev = (L - t_nat) % L
    pad = LANE - feats.shape[1]
    zf = np.pad(feats[t_nat], ((0, 0), (0, pad)))
    zr = np.pad(feats[t_rev], ((0, 0), (0, pad)))
    dl = np.tile(deltas, HYENA_ORDER)[None, :]
    w1p = jnp.pad(w1, ((0, pad), (0, 0)))
    w3r = w3.reshape(FILTER_FF, HYENA_ORDER, 2, D_HYENA)
    w3f = w3r[:, :, 0].reshape(FILTER_FF, HYENA_ORDER * D_HYENA)
    w3b = w3r[:, :, 1].reshape(FILTER_FF, HYENA_ORDER * D_HYENA)
    C = HYENA_ORDER * D_HYENA
    cb = HY_CB
    full = lambda a: pl.BlockSpec(a.shape, lambda c: (0,) * a.ndim)
    colb = lambda r: pl.BlockSpec((r, cb), lambda c: (0, c))
    zf, zr, dl = (jnp.asarray(a, F32) for a in (zf, zr, dl))
    args = (zf, zr, w1p, b1[None, :], w2, b2[None, :], w3f, w3b, sin_freq, dl)
    return pl.pallas_call(
        _filter_time_kernel,
        grid=(C // cb,),
        in_specs=[full(zf), full(zr), full(w1p), full(args[3]), full(w2), full(args[5]),
                  colb(FILTER_FF), colb(FILTER_FF), full(sin_freq), colb(1)],
        out_specs=[colb(L), colb(L)],
        out_shape=[jax.ShapeDtypeStruct((L, C), F32)] * 2,
        scratch_shapes=[pltpu.VMEM((L, FILTER_FF), F32)] * 2,
        compiler_params=pltpu.CompilerParams(dimension_semantics=("arbitrary",),
                                             vmem_limit_bytes=VMEM_LIMIT),
        name="filter_time",
    )(*args)


def _hyena(u, short_w, short_b, skip, w1, b1, w2, b2, w3, sin_freq, B, L):
    tabs = {name: jnp.asarray(t).astype(BF16) for name, t in _dft_tables(L // GRID_W, GRID_W).items()}
    kc, kn = _filter_time(w1, b1, w2, b2, w3, sin_freq, L)
    spectra = _filter_spectra(kc, kn, tabs, L)
    ncb = D_HYENA // HY_CB
    sb = short_b[None, :]
    z1 = _hyena_order(u, 0, u, ncb, short_w, sb, spectra, 0, skip[0:1], tabs, B, L, True)
    return _hyena_order(z1, 0, u, 2 * ncb, short_w, sb, spectra, ncb, skip[1:2], tabs, B, L, False)


def _attn_kernel(q_ref, k_ref, v_ref, g_ref, st_ref, o_ref, *, tk):
    G, tq = q_ref.shape[1], q_ref.shape[2]
    L = k_ref.shape[2]
    R = G * tq
    q = q_ref[0].reshape(R, HEAD_DIM)
    m = jnp.full((R, 1), -jnp.inf, F32)
    acc = jnp.zeros((R, LANE), F32)
    for c in range(L // tk):
        kc = k_ref[0, 0, c * tk:(c + 1) * tk, :]
        s = lax.dot_general(q, kc, (((1,), (1,)), ((), ())), preferred_element_type=F32)
        m_new = jnp.maximum(m, jnp.max(s, axis=-1, keepdims=True))
        p = jnp.exp2((s - m_new).astype(BF16))
        acc = jnp.exp2(m - m_new) * acc + jnp.dot(p, v_ref[0, 0, c * tk:(c + 1) * tk, :],
                                                   preferred_element_type=F32)
        m = m_new
    a_hi, a_lo = _split_bf16(acc * acc)
    ms = (jnp.dot(a_hi, st_ref[:, :LANE], preferred_element_type=F32)
          + jnp.dot(a_lo, st_ref[:, :LANE], preferred_element_type=F32))
    l_hi, l_lo = _split_bf16(acc)
    l = (jnp.dot(l_hi, st_ref[:, LANE:], preferred_element_type=F32)
         + jnp.dot(l_lo, st_ref[:, LANE:], preferred_element_type=F32))
    o = (acc * lax.rsqrt(ms + EPS * l * l))[:, :HEAD_DIM]
    o = jnp.concatenate([o[h * tq:(h + 1) * tq] for h in range(G)], axis=1)
    o_ref[0] = (o * g_ref[...]).astype(o_ref.dtype)


def _attention(q, k, v, g_out, tq, tk):
    B, H, L, _ = q.shape
    G = H // N_KV_HEADS
    st = np.zeros((LANE, 2 * LANE), np.float32)
    st[:HEAD_DIM, :LANE] = 1.0 / HEAD_DIM
    st[HEAD_DIM, LANE:] = 1.0
    st = jnp.asarray(st, dtype=BF16)
    return pl.pallas_call(
        functools.partial(_attn_kernel, tk=tk),
        grid=(B, N_KV_HEADS, L // tq),
        in_specs=[
            pl.BlockSpec((1, G, tq, HEAD_DIM), lambda b, kv, i: (b, kv, i, 0)),
            pl.BlockSpec((1, 1, L, HEAD_DIM), lambda b, kv, i: (b, kv, 0, 0)),
            pl.BlockSpec((1, 1, L, LANE), lambda b, kv, i: (b, kv, 0, 0)),
            pl.BlockSpec((1, G * HEAD_DIM), lambda b, kv, i: (0, kv)),
            pl.BlockSpec((LANE, 2 * LANE), lambda b, kv, i: (0, 0)),
        ],
        out_specs=pl.BlockSpec((1, tq, G * HEAD_DIM), lambda b, kv, i: (b, i, kv)),
        out_shape=jax.ShapeDtypeStruct((B, L, D_ATTN), BF16),
        compiler_params=pltpu.CompilerParams(dimension_semantics=("parallel", "parallel", "parallel"),
                                             vmem_limit_bytes=VMEM_LIMIT),
        name="attention",
    )(q, k, v, g_out, st)


def _memkv_kernel(m_ref, g_ref, wk_ref, wv_ref, kg_ref, k_ref, v_ref):
    h = (_rms(m_ref[0]) * g_ref[...]).astype(BF16)
    k = jnp.dot(h, wk_ref[...], preferred_element_type=F32)
    v = jnp.dot(h, wv_ref[...], preferred_element_type=F32)
    kn = [_rms(k[:, i * MEM_HEAD_DIM:(i + 1) * MEM_HEAD_DIM]) * kg_ref[...] for i in range(MEM_HEADS)]
    k_ref[0] = jnp.concatenate(kn, axis=1).astype(BF16)
    v_ref[0] = v.astype(BF16)


def _memkv(mem, g, wk_bf, wv_bf, kg):
    B, M, D = mem.shape
    return pl.pallas_call(
        _memkv_kernel,
        grid=(B,),
        in_specs=[
            pl.BlockSpec((1, M, D), lambda b: (b, 0, 0)),
            pl.BlockSpec((1, D), lambda b: (0, 0)),
            pl.BlockSpec((D, D_MEM_ATTN), lambda b: (0, 0)),
            pl.BlockSpec((D, D_MEM_ATTN), lambda b: (0, 0)),
            pl.BlockSpec((1, MEM_HEAD_DIM), lambda b: (0, 0)),
        ],
        out_specs=[pl.BlockSpec((1, M, D_MEM_ATTN), lambda b: (b, 0, 0))] * 2,
        out_shape=[jax.ShapeDtypeStruct((B, M, D_MEM_ATTN), BF16)] * 2,
        compiler_params=pltpu.CompilerParams(dimension_semantics=("parallel",),
                                             vmem_limit_bytes=VMEM_LIMIT),
        name="memkv",
    )(mem, g, wk_bf, wv_bf, kg)


def _split_bf16(a):
    hi = a.astype(BF16)
    return hi, (a - hi.astype(F32)).astype(BF16)


def _router_logits(h, wrh_ref, wrl_ref, br_ref):
    h_hi, h_lo = _split_bf16(h)
    dn = (((1,), (1,)), ((), ()))
    return (lax.dot_general(wrh_ref[...], h_hi, dn, preferred_element_type=F32)
            + lax.dot_general(wrh_ref[...], h_lo, dn, preferred_element_type=F32)
            + lax.dot_general(wrl_ref[...], h_hi, dn, preferred_element_type=F32)) + br_ref[...]


def _mix_kernel(x_ref, yh_ref, ya_ref, gh_ref, gmat_ref, wo_ref, gx_ref, wq_ref, qg_ref,
                km_ref, vm_ref, wom_ref, gmoe_ref, wrh_ref, wrl_ref, br_ref, tri_ref,
                x2_ref, route_ref, cnt_ref):
    x = x_ref[...]
    tm = x.shape[0]
    yh = yh_ref[...].astype(F32)
    ms = jnp.dot((yh * yh).astype(BF16), gmat_ref[...], preferred_element_type=F32)
    yh = yh * lax.rsqrt(ms + EPS) * gh_ref[...]
    x1 = (x
          + jnp.dot(yh.astype(BF16), wo_ref[:D_HYENA, :], preferred_element_type=F32)
          + jnp.dot(ya_ref[...], wo_ref[D_HYENA:, :], preferred_element_type=F32))
    h2 = (_rms(x1) * gx_ref[...]).astype(BF16)
    qm = jnp.dot(h2, wq_ref[...], preferred_element_type=F32)
    heads = []
    for i in range(MEM_HEADS):
        sl = slice(i * MEM_HEAD_DIM, (i + 1) * MEM_HEAD_DIM)
        qn = (_rms(qm[:, sl]) * qg_ref[...] * (MEM_HEAD_DIM ** -0.5)).astype(BF16)
        s = lax.dot_general(qn, km_ref[0, :, sl], (((1,), (1,)), ((), ())), preferred_element_type=F32)
        p = jnp.exp(s - jnp.max(s, axis=-1, keepdims=True))
        o = jnp.dot(p.astype(BF16), vm_ref[0, :, sl], preferred_element_type=F32)
        heads.append(o / jnp.sum(p, axis=-1, keepdims=True))
    om = jnp.concatenate(heads, axis=1).astype(BF16)
    x2 = x1 + jnp.dot(om, wom_ref[...], preferred_element_type=F32)
    for j in range(SLAB):
        x2_ref[pl.ds(j, tm, stride=SLAB), :] = x2[:, j * LANE:(j + 1) * LANE]

    lt = _router_logits(_rms(x2) * gmoe_ref[...], wrh_ref, wrl_ref, br_ref)
    g = [lt[i:i + 1, :] for i in range(N_GROUPS)]
    gmax = jnp.maximum(jnp.maximum(g[0], g[1]), jnp.maximum(g[2], g[3]))
    gidx = jnp.where(g[0] == gmax, 0, jnp.where(g[1] == gmax, 1, jnp.where(g[2] == gmax, 2, 3)))
    sel = []
    for j in range(EXPERTS_PER_GROUP):
        rows = [lt[N_GROUPS + EXPERTS_PER_GROUP * i + j:N_GROUPS + EXPERTS_PER_GROUP * i + j + 1, :]
                for i in range(N_GROUPS)]
        sel.append(jnp.where(gidx == 0, rows[0], jnp.where(gidx == 1, rows[1],
                                                            jnp.where(gidx == 2, rows[2], rows[3]))))
    def first_argmax(vals):
        best = jnp.maximum(jnp.maximum(vals[0], vals[1]), jnp.maximum(vals[2], vals[3]))
        return jnp.where(vals[0] == best, 0, jnp.where(vals[1] == best, 1, jnp.where(vals[2] == best, 2, 3)))

    a = first_argmax(sel)
    b = first_argmax([jnp.where(a == j, -jnp.inf, sel[j]) for j in range(EXPERTS_PER_GROUP)])
    lo, hi = jnp.minimum(a, b), jnp.maximum(a, b)
    pair = jnp.where(lo == 0, hi - 1, jnp.where(lo == 1, hi + 1, 5))
    cls = gidx * len(PAIRS) + pair

    rows_i = lax.broadcasted_iota(jnp.int32, (ROUTE_ROWS, tm), 0)
    onehot = rows_i == cls
    prefix = jnp.dot(jnp.where(onehot, 1.0, 0.0).astype(BF16), tri_ref[...], preferred_element_type=F32)
    rank = jnp.sum(jnp.where(onehot, prefix, 0.0), axis=0, keepdims=True) - 1.0
    rr = lax.broadcasted_iota(jnp.int32, (8, tm), 0)
    route_ref[...] = jnp.where(rr == 0, cls.astype(F32), jnp.where(rr == 1, rank, 0.0))
    cnt_ref[0] = prefix[:, tm - LANE:]


def _mix(x, y_hy, y_at, gh, wo_bf, gx, wq_bf, qg, km, vm, wom_bf, gmoe, wr_hi, wr_lo, br, tm):
    B, L, D = x.shape
    nt = L // tm
    T = B * L
    M = km.shape[1]
    gmat = _group_sumsq_matrix(D_HYENA, D_HYENA // HYENA_GROUPS)
    tri = jnp.asarray(np.triu(np.ones((tm, tm), np.float32)), dtype=BF16)
    const = lambda shape: pl.BlockSpec(shape, lambda i: (0,) * len(shape))
    return pl.pallas_call(
        _mix_kernel,
        grid=(B * nt,),
        in_specs=[
            pl.BlockSpec((tm, D), lambda i: (i, 0)),
            pl.BlockSpec((tm, D_HYENA), lambda i: (i, 0)),
            pl.BlockSpec((tm, D_ATTN), lambda i: (i, 0)),
            const((1, D_HYENA)), const((D_HYENA, D_HYENA)), const((D_HYENA + D_ATTN, D)),
            const((1, D)), const((D, D_MEM_ATTN)), const((1, MEM_HEAD_DIM)),
            pl.BlockSpec((1, M, D_MEM_ATTN), lambda i: (i // nt, 0, 0)),
            pl.BlockSpec((1, M, D_MEM_ATTN), lambda i: (i // nt, 0, 0)),
            const((D_MEM_ATTN, D)), const((1, D)),
            const((ROUTE_ROWS, D)), const((ROUTE_ROWS, D)), const((ROUTE_ROWS, 1)),
            const((tm, tm)),
        ],
        out_specs=[
            pl.BlockSpec((tm * SLAB, LANE), lambda i: (i, 0)),
            pl.BlockSpec((8, tm), lambda i: (0, i)),
            pl.BlockSpec((1, ROUTE_ROWS, LANE), lambda i: (i, 0, 0)),
        ],
        out_shape=[
            jax.ShapeDtypeStruct((T * SLAB, LANE), F32),
            jax.ShapeDtypeStruct((8, T), F32),
            jax.ShapeDtypeStruct((B * nt, ROUTE_ROWS, LANE), F32),
        ],
        compiler_params=pltpu.CompilerParams(dimension_semantics=("parallel",),
                                             vmem_limit_bytes=VMEM_LIMIT),
        name="mix_mem_route",
    )(x.reshape(T, D), y_hy, y_at.reshape(T, D_ATTN), gh, gmat, wo_bf, gx, wq_bf, qg,
      km, vm, wom_bf, gmoe, wr_hi, wr_lo, br, tri)


def _slab_copy(src_ref, src_tok, dst_ref, dst_tok, n, sem):
    s0 = pl.multiple_of(src_tok * SLAB, SLAB)
    d0 = pl.multiple_of(dst_tok * SLAB, SLAB)
    return pltpu.make_async_copy(src_ref.at[pl.ds(s0, n * SLAB)], dst_ref.at[pl.ds(d0, n * SLAB)], sem)


def _index_spec(rows):
    return pl.BlockSpec((1, 1, rows), lambda i: (i, 0, 0), memory_space=pltpu.SMEM)


def _scatter_rows_kernel(idx_ref, src_ref, init_ref, dst_ref, sem, *, rows):
    del init_ref

    def issue(r, carry):
        _slab_copy(src_ref, r, dst_ref, idx_ref[0, 0, r], 1, sem).start()
        return carry

    lax.fori_loop(0, rows, issue, 0, unroll=DMA_UNROLL)
    _slab_copy(src_ref, 0, dst_ref, 0, rows, sem).wait()


def _scatter_rows(src, dest, n_out, rows):
    T = dest.shape[0]
    init = jnp.zeros((n_out * SLAB, LANE), src.dtype)
    return pl.pallas_call(
        functools.partial(_scatter_rows_kernel, rows=rows),
        grid=(T // rows,),
        in_specs=[_index_spec(rows), pl.BlockSpec((rows * SLAB, LANE), lambda i: (i, 0)),
                  pl.BlockSpec(memory_space=pl.ANY)],
        out_specs=pl.BlockSpec(memory_space=pl.ANY),
        scratch_shapes=[pltpu.SemaphoreType.DMA(())],
        out_shape=jax.ShapeDtypeStruct((n_out * SLAB, LANE), src.dtype),
        input_output_aliases={2: 0},
        compiler_params=pltpu.CompilerParams(dimension_semantics=("arbitrary",),
                                             vmem_limit_bytes=VMEM_LIMIT),
        name="scatter_rows",
    )(dest.reshape(T // rows, 1, rows), src, init)


def _gather_rows_kernel(idx_ref, idx_next_ref, ys_ref, o_ref, buf_ref, sem):
    i = pl.program_id(0)
    tm = o_ref.shape[0]

    def issue(tile_idx_ref, slot):
        def body(r, carry):
            _slab_copy(ys_ref, tile_idx_ref[0, 0, r], buf_ref.at[slot], r, 1, sem.at[slot]).start()
            return carry

        lax.fori_loop(0, tm, body, 0, unroll=DMA_UNROLL)

    def step(slot):
        @pl.when(i == 0)
        def _():
            issue(idx_ref, slot)

        @pl.when(i + 1 < pl.num_programs(0))
        def _():
            issue(idx_next_ref, 1 - slot)

        _slab_copy(ys_ref, 0, buf_ref.at[slot], 0, tm, sem.at[slot]).wait()
        o_ref[...] = jnp.concatenate([buf_ref[slot, pl.ds(j, tm, stride=SLAB), :] for j in range(SLAB)],
                                     axis=1)

    for slot in range(2):
        pl.when(i % 2 == slot)(functools.partial(step, slot))


def _gather_rows(ys, dest, tm):
    T = dest.shape[0]
    D = SLAB * LANE
    n = T // tm
    idx = dest.reshape(n, 1, tm)
    next_spec = pl.BlockSpec((1, 1, tm), lambda i: (jnp.minimum(i + 1, n - 1), 0, 0), memory_space=pltpu.SMEM)
    return pl.pallas_call(
        _gather_rows_kernel,
        grid=(n,),
        in_specs=[_index_spec(tm), next_spec, pl.BlockSpec(memory_space=pl.ANY)],
        out_specs=pl.BlockSpec((tm, D), lambda i: (i, 0)),
        scratch_shapes=[pltpu.VMEM((2, tm * SLAB, LANE), F32), pltpu.SemaphoreType.DMA((2,))],
        out_shape=jax.ShapeDtypeStruct((T, D), F32),
        compiler_params=pltpu.CompilerParams(dimension_semantics=("arbitrary",),
                                             vmem_limit_bytes=VMEM_LIMIT),
        name="gather_rows",
    )(idx, idx, ys)


def _moe_kernel(lo_ref, hi_ref, valid_ref, xs_ref, gmoe_ref, wrh_ref, br_ref,
                gu_lo_ref, dn_lo_ref, gu_hi_ref, dn_hi_ref, o_ref):
    i = pl.program_id(0)
    ts = o_ref.shape[0] // SLAB

    @pl.when(valid_ref[i] > 0)
    def _():
        x2 = jnp.concatenate([xs_ref[pl.ds(j, ts, stride=SLAB), :] for j in range(SLAB)], axis=1)
        h = (_rms(x2) * gmoe_ref[...]).astype(BF16)
        lt = lax.dot_general(wrh_ref[...], h, (((1,), (1,)), ((), ())),
                             preferred_element_type=F32) + br_ref[...]
        rows = lax.broadcasted_iota(jnp.int32, lt.shape, 0)
        e_lo, e_hi = lo_ref[i], hi_ref[i]
        pick = lambda r: jnp.sum(jnp.where(rows == r, lt, 0.0), axis=0, keepdims=True)
        is_grp = rows < N_GROUPS
        gmax = jnp.max(jnp.where(is_grp, lt, -jnp.inf), axis=0, keepdims=True)
        p_grp = (jnp.exp(pick(e_lo // EXPERTS_PER_GROUP) - gmax)
                 / jnp.sum(jnp.where(is_grp, jnp.exp(lt - gmax), 0.0), axis=0, keepdims=True))
        s_lo, s_hi = pick(N_GROUPS + e_lo), pick(N_GROUPS + e_hi)
        smax = jnp.maximum(s_lo, s_hi)
        x_lo, x_hi = jnp.exp(s_lo - smax), jnp.exp(s_hi - smax)
        w_lo = p_grp * x_lo / (x_lo + x_hi)
        w_hi = p_grp * x_hi / (x_lo + x_hi)
        rw = lax.broadcasted_iota(jnp.int32, (LANE, ts), 0)
        gates = jnp.where(rw == 0, w_lo, jnp.where(rw == 1, w_hi, 0.0)).T
        y = x2
        for col, gu_ref, dn_ref in ((0, gu_lo_ref, dn_lo_ref), (1, gu_hi_ref, dn_hi_ref)):
            gu = jnp.dot(h, gu_ref[0], preferred_element_type=F32)
            act = jax.nn.silu(gu[:, :D_EXPERT]) * gu[:, D_EXPERT:]
            y = y + gates[:, col:col + 1] * jnp.dot(act.astype(BF16), dn_ref[0], preferred_element_type=F32)
        for j in range(SLAB):
            o_ref[pl.ds(j, ts, stride=SLAB), :] = y[:, j * LANE:(j + 1) * LANE]

    @pl.when(valid_ref[i] == 0)
    def _():
        o_ref[...] = jnp.zeros_like(o_ref)


def _moe(xs, tile_lo, tile_hi, tile_valid, gmoe, wr_hi, br, w_gu_bf, w_dn_bf, ts):
    n_tiles = tile_lo.shape[0]
    D = D_MODEL
    const = lambda shape: pl.BlockSpec(shape, lambda i, lo, hi, va: (0,) * len(shape))
    return pl.pallas_call(
        _moe_kernel,
        grid_spec=pltpu.PrefetchScalarGridSpec(
            num_scalar_prefetch=3,
            grid=(n_tiles,),
            in_specs=[
                pl.BlockSpec((ts * SLAB, LANE), lambda i, lo, hi, va: (i, 0)),
                const((1, D)), const((ROUTE_ROWS, D)), const((ROUTE_ROWS, 1)),
                pl.BlockSpec((1, D, 2 * D_EXPERT), lambda i, lo, hi, va: (lo[i], 0, 0)),
                pl.BlockSpec((1, D_EXPERT, D), lambda i, lo, hi, va: (lo[i], 0, 0)),
                pl.BlockSpec((1, D, 2 * D_EXPERT), lambda i, lo, hi, va: (hi[i], 0, 0)),
                pl.BlockSpec((1, D_EXPERT, D), lambda i, lo, hi, va: (hi[i], 0, 0)),
            ],
            out_specs=pl.BlockSpec((ts * SLAB, LANE), lambda i, lo, hi, va: (i, 0)),
        ),
        out_shape=jax.ShapeDtypeStruct((n_tiles * ts * SLAB, LANE), F32),
        compiler_params=pltpu.CompilerParams(dimension_semantics=("arbitrary",),
                                             vmem_limit_bytes=VMEM_LIMIT),
        name="moe_experts",
    )(tile_lo, tile_hi, tile_valid, xs, gmoe, wr_hi, br, w_gu_bf, w_dn_bf, w_gu_bf, w_dn_bf)


def _route_tables(route, counts, tm, ts):
    T = route.shape[1]
    cls = route[0].astype(jnp.int32)
    rank = route[1].astype(jnp.int32)
    cnt = counts[:, :N_CLASSES, LANE - 1].astype(jnp.int32)
    total = jnp.sum(cnt, axis=0)
    tiles_c = (total + ts - 1) // ts
    tile_start = jnp.cumsum(tiles_c) - tiles_c
    before = jnp.cumsum(cnt, axis=0) - cnt
    base = tile_start[None, :] * ts + before
    classes = jnp.arange(N_CLASSES, dtype=jnp.int32)
    pick = cls.reshape(-1, tm)[:, :, None] == classes[None, None, :]
    dest = jnp.sum(jnp.where(pick, base[:, None, :], 0), axis=-1).reshape(T) + rank
    n_tiles = T // ts + N_CLASSES
    tile_id = jnp.arange(n_tiles, dtype=jnp.int32)
    used = jnp.sum(tiles_c)
    ends = jnp.cumsum(tiles_c)
    tcls = jnp.sum(jnp.minimum(tile_id, used - 1)[:, None] >= ends[None, :], axis=1).astype(jnp.int32)
    tcls = jnp.minimum(tcls, N_CLASSES - 1)
    grp, pr = tcls // len(PAIRS), tcls % len(PAIRS)
    pair_lo = (pr >= 3).astype(jnp.int32) + (pr >= 5).astype(jnp.int32)
    pair_hi = jnp.where(pr < 3, pr + 1, jnp.where(pr < 5, pr - 1, 3))
    tile_lo = grp * EXPERTS_PER_GROUP + pair_lo
    tile_hi = grp * EXPERTS_PER_GROUP + pair_hi
    return dest, tile_lo, tile_hi, (tile_id < used).astype(jnp.int32)


def _moe_block(x2s, route, counts, gmoe, wr_hi, br, w_gu_bf, w_dn_bf, tm):
    T = route.shape[1]
    ts = _pick_tile(T, 256)
    dest, tile_lo, tile_hi, tile_valid = _route_tables(route, counts, tm, ts)
    n_tiles = T // ts + N_CLASSES
    xs = _scatter_rows(x2s, dest, n_tiles * ts, _pick_tile(T, 1024))
    ys = _moe(xs, tile_lo, tile_hi, tile_valid, gmoe, wr_hi, br, w_gu_bf, w_dn_bf, ts)
    return _gather_rows(ys, dest, tm)


def _pick_tile(n, pref):
    t = min(pref, n)
    while n % t:
        t //= 2
    return t


def kernel(x, mem, norm_mix, w_in, hyena_short_w, hyena_short_b, filt_w1, filt_b1, filt_w2, filt_b2, filt_w3, filt_sin_freq, hyena_skip, attn_q_norm, attn_k_norm, out_norm_hyena, out_norm_attn, w_out, norm_xattn, norm_mem, w_q_mem, w_k_mem, w_v_mem, mem_q_norm, mem_k_norm, w_o_mem, norm_moe, w_router_grp, b_router_grp, w_router_exp, b_router_exp, w_gate, w_up, w_down):
    B, L, D = x.shape
    T = B * L
    NR = L // GRID_W
    depth = norm_mix.shape[0]
    tm = _pick_tile(L, 1024)
    tq = _pick_tile(L, 512)
    tk = _pick_tile(L, 512)
    cos_t, sin_t = _rope_tables(L)

    def to_column_major(a):
        return a.reshape(B, NR, GRID_W, -1).transpose(0, 2, 1, 3).reshape(T, -1)

    def to_row_major(a):
        return a.reshape(B, GRID_W, NR, -1).transpose(0, 2, 1, 3).reshape(T, -1)

    for i in range(depth):
        qkg = jnp.concatenate([jnp.tile(attn_q_norm[i], N_Q_HEADS), jnp.tile(attn_k_norm[i], N_KV_HEADS)])[None, :]
        u, q, k, v = _inproj(x, norm_mix[i][None, :], w_in[i].astype(BF16), cos_t, sin_t, qkg, tm)
        y_hy = to_row_major(_hyena(to_column_major(u), hyena_short_w[i], hyena_short_b[i], hyena_skip[i],
                                   filt_w1[i], filt_b1[i], filt_w2[i], filt_b2[i], filt_w3[i],
                                   filt_sin_freq[i], B, L))
        y_at = _attention(q, k, v, out_norm_attn[i][None, :], tq, tk)
        km, vm = _memkv(mem, norm_mem[i][None, :], w_k_mem[i].astype(BF16), w_v_mem[i].astype(BF16),
                        mem_k_norm[i][None, :])
        wr = jnp.concatenate([w_router_grp[i], w_router_exp[i]], axis=1).T
        wr = jnp.pad(wr, ((0, ROUTE_ROWS - wr.shape[0]), (0, 0)))
        wr_hi, wr_lo = _split_bf16(wr)
        br = jnp.pad(jnp.concatenate([b_router_grp[i], b_router_exp[i]]), (0, ROUTE_ROWS - N_GROUPS - N_EXPERTS))
        gmoe = norm_moe[i][None, :]
        x2s, route, counts = _mix(x, y_hy, y_at, out_norm_hyena[i][None, :], w_out[i].astype(BF16),
                                  norm_xattn[i][None, :], w_q_mem[i].astype(BF16), mem_q_norm[i][None, :],
                                  km, vm, w_o_mem[i].astype(BF16), gmoe, wr_hi, wr_lo, br[:, None], tm)
        w_gu = jnp.concatenate([w_gate[i], w_up[i]], axis=-1).astype(BF16)
        x = _moe_block(x2s, route, counts, gmoe, wr_hi, br[:, None], w_gu, w_down[i].astype(BF16),
                       tm).reshape(B, L, D)
    return x
```
